```python
import math
import jax
import jax.numpy as jnp
from jax import lax
import numpy as np

D_MODEL = 2048
BATCH = 4
SEQ = 2048
DEPTH = 4
DEC_BATCH = 32
DEC_SEQ = 1
PAST_LEN = 16384
PAGE_SIZE = 128

N_MIXERS = 3
N_A = (DEPTH + 2) // 3
N_B = (DEPTH + 1) // 3
N_C = DEPTH // 3
HDA = 128
HA = D_MODEL // HDA
KVA = 4
HDB = 128
HB = D_MODEL // (2 * HDB)
KVB = 4
HDC = 64
HC = D_MODEL // HDC
KVC = 8
WINDOW = 128
N_MEM = 256
CH = 4
CHD = 128
D_FF = (((8 * D_MODEL + 2) // 3 + 255) // 256) * 256
QBLOCK = 128
EPS = 1e-6
NEG = -1e30
F32 = jnp.float32

kernel_name = 'hybrid_fox_diff_swa_decoder_step'


def rmsnorm(x, g):
    xf = x.astype(F32)
    y = xf * lax.rsqrt(jnp.mean(xf * xf, axis=-1, keepdims=True) + EPS)
    return (y * g.astype(F32)).astype(x.dtype)


def alibi_slopes(n):
    return jnp.exp2(-8.0 * jnp.arange(1, n + 1, dtype=F32) / n)


def gather_pages(pool, layer, page_table):
    g = pool[layer, page_table]
    return g.reshape((g.shape[0], g.shape[1] * g.shape[2]) + g.shape[3:])


def masked_softmax(s, mask):
    return jax.nn.softmax(jnp.where(mask, s, NEG), axis=-1)


def fox_in(h, w_in, b_f):
    B, T, _ = h.shape
    nq, nkv = HA * HDA, KVA * HDA
    z = h @ w_in
    q = z[..., :nq].reshape(B, T, KVA, HA // KVA, HDA)
    k = z[..., nq:nq + nkv].reshape(B, T, KVA, HDA)
    v = z[..., nq + nkv:nq + 2 * nkv].reshape(B, T, KVA, HDA)
    logf = jax.nn.log_sigmoid(z[..., nq + 2 * nkv:].astype(F32) + b_f.astype(F32))
    return q, k, v, logf


def fox_prompt(q, k, v, logf):
    B, S, KV, G, Dh = q.shape
    nb = S // QBLOCK
    c = jnp.cumsum(logf, axis=1).reshape(B, S, KV, G).transpose(0, 2, 3, 1)
    qb = jnp.moveaxis(q.reshape(B, nb, QBLOCK, KV, G, Dh), 1, 0)
    cb = jnp.moveaxis(c.reshape(B, KV, G, nb, QBLOCK), 3, 0)
    kpos = jnp.arange(S)
    scale = Dh ** -0.5

    def block(args):
        qi, ci, n = args
        qpos = n * QBLOCK + jnp.arange(QBLOCK)
        s = jnp.einsum('bqkgd,bskd->bkgqs', qi, k, preferred_element_type=F32) * scale
        s = s + ci[..., :, None] - c[..., None, :]
        p = masked_softmax(s, kpos[None, :] <= qpos[:, None])
        return jnp.einsum('bkgqs,bskd->bqkgd', p.astype(v.dtype), v)

    o = lax.map(block, (qb, cb, jnp.arange(nb)))
    return jnp.moveaxis(o, 0, 1).reshape(B, S, KV * G * Dh)


def fox_sample(q, k, v, logf, k_past, v_past, logf_past):
    B, T, KV, G, Dh = q.shape
    P = k_past.shape[1]
    lf = jnp.concatenate([logf_past.astype(F32), logf], axis=1)
    c = jnp.cumsum(lf, axis=1).reshape(B, P + T, KV, G).transpose(0, 2, 3, 1)
    scale = Dh ** -0.5
    s = jnp.concatenate([
        jnp.einsum('btkgd,bskd->bkgts', q, k_past, preferred_element_type=F32),
        jnp.einsum('btkgd,bskd->bkgts', q, k, preferred_element_type=F32)], axis=-1) * scale
    s = s + c[..., P:, None] - c[..., None, :]
    mask = jnp.arange(P + T)[None, :] <= (P + jnp.arange(T))[:, None]
    p = masked_softmax(s, mask)
    o = (jnp.einsum('bkgts,bskd->btkgd', p[..., :P].astype(v_past.dtype), v_past)
         + jnp.einsum('bkgts,bskd->btkgd', p[..., P:].astype(v.dtype), v))
    return o.reshape(B, T, KV * G * Dh)


def diff_in(h, w_in):
    B, T, _ = h.shape
    nq, nkv = HB * 2 * HDB, KVB * 2 * HDB
    z = h @ w_in
    q = z[..., :nq].reshape(B, T, KVB, HB // KVB, 2, HDB)
    k = z[..., nq:nq + nkv].reshape(B, T, KVB, 2 * HDB)
    v = z[..., nq + nkv:].reshape(B, T, KVB, 2 * HDB)
    return q, k, v


def diff_lambda(lq1, lk1, lq2, lk2, lam_init):
    return (jnp.exp(jnp.sum(lq1.astype(F32) * lk1.astype(F32)))
            - jnp.exp(jnp.sum(lq2.astype(F32) * lk2.astype(F32))) + lam_init)


def diff_weights(s, mask, lam):
    p = masked_softmax(s, mask)
    return p[0] - lam * p[1]


def diff_prompt(q, k, v, lam, slopes):
    B, S, KV, G, _, Dh = q.shape
    nb = S // QBLOCK
    k2 = k.reshape(B, S, KV, 2, Dh)
    qb = jnp.moveaxis(q.reshape(B, nb, QBLOCK, KV, G, 2, Dh), 1, 0)
    kpos = jnp.arange(S)
    sl = slopes.reshape(KV, G)[:, :, None, None]
    scale = Dh ** -0.5

    def block(args):
        qi, n = args
        qpos = n * QBLOCK + jnp.arange(QBLOCK)
        rel = qpos[:, None] - kpos[None, :]
        s = jnp.einsum('bqkgmd,bskmd->mbkgqs', qi, k2, preferred_element_type=F32) * scale
        s = s - sl * rel.astype(F32)
        w = diff_weights(s, rel >= 0, lam)
        return jnp.einsum('bkgqs,bskd->bqkgd', w.astype(v.dtype), v)

    o = lax.map(block, (qb, jnp.arange(nb)))
    return jnp.moveaxis(o, 0, 1).reshape(B, S, KV, G, 2 * Dh)


def diff_sample(q, k, v, k_past, v_past, lam, slopes):
    B, T, KV, G, _, Dh = q.shape
    P = k_past.shape[1]
    kp2 = k_past.reshape(B, P, KV, 2, Dh)
    kn2 = k.reshape(B, T, KV, 2, Dh)
    sl = slopes.reshape(KV, G)[:, :, None, None]
    scale = Dh ** -0.5
    s = jnp.concatenate([
        jnp.einsum('btkgmd,bskmd->mbkgts', q, kp2, preferred_element_type=F32),
        jnp.einsum('btkgmd,bskmd->mbkgts', q, kn2, preferred_element_type=F32)], axis=-1) * scale
    rel = (P + jnp.arange(T))[:, None] - jnp.arange(P + T)[None, :]
    s = s - sl * rel.astype(F32)
    w = diff_weights(s, rel >= 0, lam)
    o = (jnp.einsum('bkgts,bskd->btkgd', w[..., :P].astype(v_past.dtype), v_past)
         + jnp.einsum('bkgts,bskd->btkgd', w[..., P:].astype(v.dtype), v))
    return o


def diff_out(o, g, lam_init):
    B, T = o.shape[:2]
    return (rmsnorm(o, g) * (1.0 - lam_init)).reshape(B, T, -1)


def swa_in(h, w_in, b_in):
    B, T, _ = h.shape
    nq, nkv = HC * HDC, KVC * HDC
    z = h @ w_in + b_in
    q = z[..., :nq].reshape(B, T, KVC, HC // KVC, HDC)
    k = z[..., nq:nq + nkv].reshape(B, T, KVC, HDC)
    v = z[..., nq + nkv:].reshape(B, T, KVC, HDC)
    return q, k, v


def sink_softmax(s, mask, sinks):
    s = jnp.where(mask, s, NEG)
    snk = jnp.broadcast_to(sinks.astype(F32)[:, :, None, None], s.shape[:-1] + (1,))
    p = jax.nn.softmax(jnp.concatenate([s, snk], axis=-1), axis=-1)
    return p[..., :-1]


def swa_prompt(q, k, v, sinks, slopes):
    B, S, KV, G, Dh = q.shape
    nb = S // WINDOW
    qb = q.reshape(B, nb, WINDOW, KV, G, Dh)

    def band(x):
        xb = x.reshape(B, nb, WINDOW, KV, Dh)
        prev = jnp.pad(xb, ((0, 0), (1, 0), (0, 0), (0, 0), (0, 0)))[:, :-1]
        return jnp.concatenate([prev, xb], axis=2)

    kk, vv = band(k), band(v)
    i = jnp.arange(WINDOW)[:, None]
    j = jnp.arange(2 * WINDOW)[None, :]
    rel = WINDOW + i - j
    kpos = (jnp.arange(nb)[:, None, None] - 1) * WINDOW + j[None]
    mask = (rel >= 0) & (rel <= WINDOW) & (kpos >= 0)
    sl = slopes.reshape(KV, G)[:, :, None, None]
    s = jnp.einsum('bnqkgd,bnskd->bnkgqs', qb, kk, preferred_element_type=F32) * (Dh ** -0.5)
    s = s - sl * rel.astype(F32)
    p = sink_softmax(s, mask[:, None, None], sinks.reshape(KV, G))
    o = jnp.einsum('bnkgqs,bnskd->bnqkgd', p.astype(vv.dtype), vv)
    return o.reshape(B, S, KV * G * Dh)


def swa_sample(q, k, v, k_buf, v_buf, sinks, slopes):
    B, T, KV, G, Dh = q.shape
    W = k_buf.shape[1]
    kk = jnp.concatenate([k_buf.astype(k.dtype), k], axis=1)
    vv = jnp.concatenate([v_buf.astype(v.dtype), v], axis=1)
    rel = (W + jnp.arange(T))[:, None] - jnp.arange(W + T)[None, :]
    mask = (rel >= 0) & (rel <= WINDOW)
    sl = slopes.reshape(KV, G)[:, :, None, None]
    s = jnp.einsum('btkgd,bskd->bkgts', q, kk, preferred_element_type=F32) * (Dh ** -0.5)
    s = s - sl * rel.astype(F32)
    p = sink_softmax(s, mask, sinks.reshape(KV, G))
    o = jnp.einsum('bkgts,bskd->btkgd', p.astype(vv.dtype), vv).reshape(B, T, KV * G * Dh)
    return o, kk[:, -WINDOW:], vv[:, -WINDOW:]


def mem_kv(mem, g, w_kv):
    B, M, _ = mem.shape
    z = rmsnorm(mem, g) @ w_kv
    return (z[..., :CH * CHD].reshape(B, M, CH, CHD), z[..., CH * CHD:].reshape(B, M, CH, CHD))


def cross_attn(h, mk, mv, w_q, w_o):
    B, T, _ = h.shape
    q = (h @ w_q).reshape(B, T, CH, CHD)
    s = jnp.einsum('btnd,bmnd->bntm', q, mk.astype(q.dtype), preferred_element_type=F32) * (CHD ** -0.5)
    p = jax.nn.softmax(s, axis=-1)
    o = jnp.einsum('bntm,bmnd->btnd', p.astype(q.dtype), mv.astype(q.dtype)).reshape(B, T, CH * CHD)
    return o @ w_o


def swiglu(h, w_gu, w_down):
    z = h @ w_gu
    return (jax.nn.silu(z[..., :D_FF]) * z[..., D_FF:]) @ w_down


def setup_inputs(seed: int = 0) -> dict:
    key = jax.random.key(seed)
    keys = iter(jax.random.split(key, 48))

    def nrm(shape, scale=1.0):
        return jax.random.normal(next(keys), shape, F32) * scale

    def gain(shape):
        return 1.0 + nrm(shape, 0.02)

    n_pages = PAST_LEN // PAGE_SIZE
    n_used = DEC_BATCH * n_pages
    n_pool = n_used + (n_used + 3) // 4
    d = D_MODEL
    a_in = HA * HDA + 2 * KVA * HDA + HA
    b_in = HB * 2 * HDB + 2 * KVB * 2 * HDB
    c_in = HC * HDC + 2 * KVC * HDC
    x_prompt = nrm((BATCH, SEQ, d))
    x_sample = nrm((DEC_BATCH, DEC_SEQ, d))
    mem_prompt = nrm((BATCH, N_MEM, d))
    cache_fox_k = nrm((N_A, n_pool, PAGE_SIZE, KVA, HDA))
    cache_fox_v = nrm((N_A, n_pool, PAGE_SIZE, KVA, HDA))
    cache_fox_logf = jax.nn.log_sigmoid(nrm((N_A, n_pool, PAGE_SIZE, HA)) + 3.0)
    cache_diff_k = nrm((N_B, n_pool, PAGE_SIZE, KVB, 2 * HDB))
    cache_diff_v = nrm((N_B, n_pool, PAGE_SIZE, KVB, 2 * HDB))
    state_swa_k = nrm((N_C, DEC_BATCH, WINDOW, KVC, HDC))
    state_swa_v = nrm((N_C, DEC_BATCH, WINDOW, KVC, HDC))
    cache_mem_k = nrm((DEPTH, DEC_BATCH, N_MEM, CH, CHD))
    cache_mem_v = nrm((DEPTH, DEC_BATCH, N_MEM, CH, CHD))
    page_table = jax.random.permutation(next(keys), n_pool)[:n_used].reshape(DEC_BATCH, n_pages).astype(jnp.int32)
    return {
        'x_prompt': x_prompt, 'x_sample': x_sample, 'mem_prompt': mem_prompt,
        'cache_fox_k': cache_fox_k, 'cache_fox_v': cache_fox_v, 'cache_fox_logf': cache_fox_logf,
        'cache_diff_k': cache_diff_k, 'cache_diff_v': cache_diff_v,
        'state_swa_k': state_swa_k, 'state_swa_v': state_swa_v,
        'cache_mem_k': cache_mem_k, 'cache_mem_v': cache_mem_v,
        'page_table': page_table,
        'norm_mix': gain((DEPTH, d)),
        'w_a_in': nrm((N_A, d, a_in), d ** -0.5),
        'b_a_f': 3.0 + nrm((N_A, HA), 0.5),
        'w_a_out': nrm((N_A, HA * HDA, d), (HA * HDA) ** -0.5),
        'w_b_in': nrm((N_B, d, b_in), d ** -0.5),
        'lam_q1': nrm((N_B, HDB), 0.1), 'lam_k1': nrm((N_B, HDB), 0.1),
        'lam_q2': nrm((N_B, HDB), 0.1), 'lam_k2': nrm((N_B, HDB), 0.1),
        'subln_b': gain((N_B, 2 * HDB)),
        'w_b_out': nrm((N_B, HB * 2 * HDB, d), (HB * 2 * HDB) ** -0.5),
        'w_c_in': nrm((N_C, d, c_in), d ** -0.5),
        'b_c_in': nrm((N_C, c_in), 0.02),
        'sinks_c': nrm((N_C, HC)),
        'w_c_out': nrm((N_C, HC * HDC, d), (HC * HDC) ** -0.5),
        'norm_cross': gain((DEPTH, d)), 'norm_mem': gain((DEPTH, d)),
        'w_x_q': nrm((DEPTH, d, CH * CHD), d ** -0.5),
        'w_x_kv': nrm((DEPTH, d, 2 * CH * CHD), d ** -0.5),
        'w_x_out': nrm((DEPTH, CH * CHD, d), (CH * CHD) ** -0.5),
        'norm_ffn': gain((DEPTH, d)),
        'w_ff_gu': nrm((DEPTH, d, 2 * D_FF), d ** -0.5),
        'w_ff_down': nrm((DEPTH, D_FF, d), D_FF ** -0.5),
        'norm_final': gain((d,)),
    }


def reference(x_prompt, x_sample, mem_prompt, cache_fox_k, cache_fox_v, cache_fox_logf,
              cache_diff_k, cache_diff_v, state_swa_k, state_swa_v, cache_mem_k, cache_mem_v,
              page_table, norm_mix, w_a_in, b_a_f, w_a_out, w_b_in, lam_q1, lam_k1, lam_q2, lam_k2,
              subln_b, w_b_out, w_c_in, b_c_in, sinks_c, w_c_out, norm_cross, norm_mem,
              w_x_q, w_x_kv, w_x_out, norm_ffn, w_ff_gu, w_ff_down, norm_final):
    xp, xs = x_prompt, x_sample
    pfk, pfv, pfl, sfk, sfv, sfl = [], [], [], [], [], []
    pdk, pdv, sdk, sdv = [], [], [], []
    pck, pcv, sck, scv = [], [], [], []
    pmk, pmv = [], []
    slopes_b = alibi_slopes(HB)
    slopes_c = alibi_slopes(HC)
    ia = ib = ic = 0
    for i in range(DEPTH):
        hp = rmsnorm(xp, norm_mix[i])
        hs = rmsnorm(xs, norm_mix[i])
        kind = i % N_MIXERS
        if kind == 0:
            qp, kp, vp, lfp = fox_in(hp, w_a_in[ia], b_a_f[ia])
            qs, ks, vs, lfs = fox_in(hs, w_a_in[ia], b_a_f[ia])
            op = fox_prompt(qp, kp, vp, lfp)
            os_ = fox_sample(qs, ks, vs, lfs,
                             gather_pages(cache_fox_k, ia, page_table),
                             gather_pages(cache_fox_v, ia, page_table),
                             gather_pages(cache_fox_logf, ia, page_table))
            xp = xp + op @ w_a_out[ia]
            xs = xs + os_ @ w_a_out[ia]
            pfk.append(kp); pfv.append(vp); pfl.append(lfp)
            sfk.append(ks); sfv.append(vs); sfl.append(lfs)
            ia += 1
        elif kind == 1:
            lam_init = 0.8 - 0.6 * math.exp(-0.3 * i)
            lam = diff_lambda(lam_q1[ib], lam_k1[ib], lam_q2[ib], lam_k2[ib], lam_init)
            qp, kp, vp = diff_in(hp, w_b_in[ib])
            qs, ks, vs = diff_in(hs, w_b_in[ib])
            op = diff_prompt(qp, kp, vp, lam, slopes_b)
            os_ = diff_sample(qs, ks, vs,
                              gather_pages(cache_diff_k, ib, page_table),
                              gather_pages(cache_diff_v, ib, page_table), lam, slopes_b)
            xp = xp + diff_out(op, subln_b[ib], lam_init) @ w_b_out[ib]
            xs = xs + diff_out(os_, subln_b[ib], lam_init) @ w_b_out[ib]
            pdk.append(kp); pdv.append(vp); sdk.append(ks); sdv.append(vs)
            ib += 1
        else:
            qp, kp, vp = swa_in(hp, w_c_in[ic], b_c_in[ic])
            qs, ks, vs = swa_in(hs, w_c_in[ic], b_c_in[ic])
            op = swa_prompt(qp, kp, vp, sinks_c[ic], slopes_c)
            os_, nbk, nbv = swa_sample(qs, ks, vs, state_swa_k[ic], state_swa_v[ic], sinks_c[ic], slopes_c)
            xp = xp + op @ w_c_out[ic]
            xs = xs + os_ @ w_c_out[ic]
            pck.append(kp[:, -WINDOW:]); pcv.append(vp[:, -WINDOW:])
            sck.append(nbk); scv.append(nbv)
            ic += 1
        mkp, mvp = mem_kv(mem_prompt, norm_mem[i], w_x_kv[i])
        xp = xp + cross_attn(rmsnorm(xp, norm_cross[i]), mkp, mvp, w_x_q[i], w_x_out[i])
        xs = xs + cross_attn(rmsnorm(xs, norm_cross[i]), cache_mem_k[i], cache_mem_v[i], w_x_q[i], w_x_out[i])
        pmk.append(mkp); pmv.append(mvp)
        xp = xp + swiglu(rmsnorm(xp, norm_ffn[i]), w_ff_gu[i], w_ff_down[i])
        xs = xs + swiglu(rmsnorm(xs, norm_ffn[i]), w_ff_gu[i], w_ff_down[i])
    y_prompt = rmsnorm(xp, norm_final)
    y_sample = rmsnorm(xs, norm_final)
    return (y_prompt, y_sample,
            jnp.stack(pfk), jnp.stack(pfv), jnp.stack(pfl),
            jnp.stack(pdk), jnp.stack(pdv),
            jnp.stack(pck), jnp.stack(pcv),
            jnp.stack(pmk), jnp.stack(pmv),
            jnp.stack(sfk), jnp.stack(sfv), jnp.stack(sfl),
            jnp.stack(sdk), jnp.stack(sdv),
            jnp.stack(sck), jnp.stack(scv))
```

```python
import functools
import math

import jax
import jax.numpy as jnp
from jax import lax
from jax.experimental import pallas as pl
from jax.experimental.pallas import tpu as pltpu

F32 = jnp.float32
BF16 = jnp.bfloat16
EPS = 1e-6
NEG = -1e30
LANES = 128
VMEM_LIMIT_BYTES = 56 * 1024 * 1024
PAGES_PER_STEP = 8
NT_DIMS = (((1,), (1,)), ((), ()))


def _params(*sem):
    return pltpu.CompilerParams(dimension_semantics=sem, vmem_limit_bytes=VMEM_LIMIT_BYTES)


def _rms(x, g):
    return x * lax.rsqrt(jnp.mean(x * x, axis=-1, keepdims=True) + EPS) * g


def _log_sigmoid(x):
    return jnp.minimum(x, 0.0) - jnp.log(1.0 + jnp.exp(-jnp.abs(x)))


def _split3(x):
    hi = x.astype(BF16)
    r1 = x - hi.astype(F32)
    mid = r1.astype(BF16)
    lo = (r1 - mid.astype(F32)).astype(BF16)
    return hi, mid, lo


def _mm_kernel(*refs, has_gain, has_bias, has_res, glu, stage_x, tm, rc):
    refs = list(refs)
    x_ref = refs.pop(0)
    g_ref = refs.pop(0) if has_gain else None
    w_ref = refs.pop(0)
    w2_ref = refs.pop(0) if glu else None
    b_ref = refs.pop(0) if has_bias else None
    r_ref = refs.pop(0) if has_res else None
    o_ref = refs.pop(0)
    xs_ref = refs.pop(0) if stage_x else None

    if stage_x:
        @pl.when(pl.program_id(1) == 0)
        def _():
            def body(c, carry):
                r0 = pl.multiple_of(c * rc, rc)
                xc = x_ref[pl.ds(r0, rc), :].astype(F32)
                if has_gain:
                    xc = _rms(xc, g_ref[...])
                xs_ref[pl.ds(r0, rc), :] = xc.astype(BF16)
                return carry

            lax.fori_loop(0, tm // rc, body, 0)

        xb = xs_ref[...]
    else:
        xb = x_ref[...]
    acc = jnp.dot(xb, w_ref[...].astype(BF16), preferred_element_type=F32)
    if glu:
        up = jnp.dot(xb, w2_ref[...].astype(BF16), preferred_element_type=F32)
        acc = acc * jax.nn.sigmoid(acc) * up
    if has_bias:
        acc = acc + b_ref[...]
    if has_res:
        acc = r_ref[...] + acc
    o_ref[...] = acc.astype(o_ref.dtype)


def _mm(x, w, layer, *, gain=None, bias=None, residual=None, glu=False, out_dtype=F32, tm=1024, tn=512):
    M, K = x.shape
    N = w.shape[-1] // (2 if glu else 1)
    tm = min(tm, M)
    assert M % tm == 0 and w.shape[-2] == K
    nj = pl.cdiv(N, tn)
    assert not glu or N % tn == 0
    stage_x = x.dtype != BF16 or gain is not None
    rc = min(tm, 128)
    in_specs = [pl.BlockSpec((tm, K), lambda i, j: (i, 0))]
    args = [x]
    if gain is not None:
        in_specs.append(pl.BlockSpec((1, K), lambda i, j: (0, 0)))
        args.append(gain.reshape(1, K).astype(F32))
    in_specs.append(pl.BlockSpec((None, K, tn), lambda i, j: (layer, 0, j)))
    args.append(w)
    if glu:
        in_specs.append(pl.BlockSpec((None, K, tn), lambda i, j: (layer, 0, j + nj)))
        args.append(w)
    if bias is not None:
        in_specs.append(pl.BlockSpec((1, tn), lambda i, j: (0, j)))
        args.append(bias.reshape(1, N).astype(F32))
    if residual is not None:
        in_specs.append(pl.BlockSpec((tm, tn), lambda i, j: (i, j)))
        args.append(residual)
    kern = functools.partial(_mm_kernel, has_gain=gain is not None, has_bias=bias is not None,
                             has_res=residual is not None, glu=glu, stage_x=stage_x, tm=tm, rc=rc)
    return pl.pallas_call(
        kern,
        grid=(M // tm, nj),
        in_specs=in_specs,
        out_specs=pl.BlockSpec((tm, tn), lambda i, j: (i, j)),
        out_shape=jax.ShapeDtypeStruct((M, N), out_dtype),
        scratch_shapes=[pltpu.VMEM((tm, K), BF16)] if stage_x else [],
        compiler_params=_params("arbitrary", "arbitrary"),
    )(*args)


def _rmsnorm_kernel(x_ref, g_ref, o_ref):
    o_ref[...] = _rms(x_ref[...], g_ref[...])


def _rmsnorm(x, g, tm=256):
    M, K = x.shape
    tm = min(tm, M)
    return pl.pallas_call(
        _rmsnorm_kernel,
        grid=(M // tm,),
        in_specs=[pl.BlockSpec((tm, K), lambda i: (i, 0)), pl.BlockSpec((1, K), lambda i: (0, 0))],
        out_specs=pl.BlockSpec((tm, K), lambda i: (i, 0)),
        out_shape=jax.ShapeDtypeStruct((M, K), F32),
        compiler_params=_params("arbitrary"),
    )(x, g.reshape(1, K))


def _online_update(s, v_b, m_ref, l_ref, acc_ref, idx, col0, dv):
    m_prev = m_ref[idx]
    m_new = jnp.maximum(m_prev, jnp.max(s, axis=1, keepdims=True))
    alpha = jnp.exp(m_prev - m_new)
    p = jnp.exp(s - m_new)
    l_ref[idx] = alpha * l_ref[idx] + jnp.sum(p, axis=1, keepdims=True)
    acc_ref[:, col0:col0 + dv] = alpha * acc_ref[:, col0:col0 + dv] + jnp.dot(
        p.astype(BF16), v_b, preferred_element_type=F32)
    m_ref[idx] = m_new


def _causal_mask(qi, ki, tq, tk):
    row = lax.broadcasted_iota(jnp.int32, (tq, tk), 0) + qi * tq
    col = lax.broadcasted_iota(jnp.int32, (tq, tk), 1) + ki * tk
    return col <= row


def _fox_gate_kernel(zg_ref, b_ref, lf_ref, c_ref, *, S):
    r = lax.broadcasted_iota(jnp.int32, (LANES, LANES), 0)
    c_ = lax.broadcasted_iota(jnp.int32, (LANES, LANES), 1)
    tri = (r <= c_).astype(BF16)
    H = zg_ref.shape[0]
    carry = jnp.zeros((H, 1), F32)
    for t in range(S // LANES):
        sl = slice(t * LANES, (t + 1) * LANES)
        lf = _log_sigmoid(zg_ref[:, sl] + b_ref[...])
        lf_ref[:, sl] = lf
        hi, mid, lo = _split3(lf)
        y = jnp.dot(jnp.concatenate([hi, mid, lo], axis=0), tri, preferred_element_type=F32)
        c = y[0:H] + y[H:2 * H] + y[2 * H:3 * H] + carry
        c_ref[:, sl] = c
        carry = c[:, LANES - 1:LANES]


def _fox_gate(zg_t, b_f):
    B, H, S = zg_t.shape
    spec = pl.BlockSpec((None, H, S), lambda b: (b, 0, 0))
    return pl.pallas_call(
        functools.partial(_fox_gate_kernel, S=S),
        grid=(B,),
        in_specs=[spec, pl.BlockSpec((H, 1), lambda b: (0, 0))],
        out_specs=[spec, spec],
        out_shape=[jax.ShapeDtypeStruct((B, H, S), F32)] * 2,
        compiler_params=_params("arbitrary"),
    )(zg_t, b_f.reshape(H, 1))


def _fox_flash_kernel(q_ref, k_ref, v_ref, ck_ref, o_ref, m_ref, l_ref, acc_ref, *, G, D, tq, tk, nk, scale):
    qi, ki = pl.program_id(2), pl.program_id(3)

    @pl.when(ki == 0)
    def _():
        m_ref[...] = jnp.full(m_ref.shape, NEG, F32)
        l_ref[...] = jnp.zeros(l_ref.shape, F32)
        acc_ref[...] = jnp.zeros(acc_ref.shape, F32)

    @pl.when(ki <= qi)
    def _():
        k_b = k_ref[...].astype(BF16)
        v_b = v_ref[...].astype(BF16)
        mask = _causal_mask(qi, ki, tq, tk)
        for g in range(G):
            q_b = (q_ref[:, g * D:(g + 1) * D] * scale).astype(BF16)
            s = lax.dot_general(q_b, k_b, NT_DIMS, preferred_element_type=F32)
            s = jnp.where(mask, s - ck_ref[g:g + 1, :], NEG)
            _online_update(s, v_b, m_ref, l_ref, acc_ref, g, g * D, D)

    @pl.when(ki == nk - 1)
    def _():
        for g in range(G):
            o_ref[:, g * D:(g + 1) * D] = (acc_ref[:, g * D:(g + 1) * D] / l_ref[g]).astype(o_ref.dtype)


def _fox_flash(z, c_t, B, S, *, H, KV, D, t=512):
    G = H // KV
    t = min(t, S)
    nq = S // t
    kern = functools.partial(_fox_flash_kernel, G=G, D=D, tq=t, tk=t, nk=nq, scale=D ** -0.5)
    return pl.pallas_call(
        kern,
        grid=(B, KV, nq, nq),
        in_specs=[
            pl.BlockSpec((t, G * D), lambda b, j, qi, ki: (b * nq + qi, j)),
            pl.BlockSpec((t, D), lambda b, j, qi, ki: (b * nq + jnp.minimum(ki, qi), H + j)),
            pl.BlockSpec((t, D), lambda b, j, qi, ki: (b * nq + jnp.minimum(ki, qi), H + KV + j)),
            pl.BlockSpec((None, None, G, t), lambda b, j, qi, ki: (b, j, 0, jnp.minimum(ki, qi))),
        ],
        out_specs=pl.BlockSpec((t, G * D), lambda b, j, qi, ki: (b * nq + qi, j)),
        out_shape=jax.ShapeDtypeStruct((B * S, H * D), BF16),
        scratch_shapes=[pltpu.VMEM((G, t, 1), F32), pltpu.VMEM((G, t, 1), F32), pltpu.VMEM((t, G * D), F32)],
        compiler_params=_params("arbitrary", "arbitrary", "arbitrary", "arbitrary"),
    )(z, z, z, c_t)


def _diff_lambda(lam_ref, lam_init):
    a = jnp.sum(lam_ref[0:1, :] * lam_ref[1:2, :], axis=1, keepdims=True)
    b = jnp.sum(lam_ref[2:3, :] * lam_ref[3:4, :], axis=1, keepdims=True)
    return jnp.exp(a) - jnp.exp(b) + lam_init


def _diff_flash_kernel(sl_ref, q_ref, k_ref, v_ref, lam_ref, g_ref, o_ref, m_ref, l_ref, acc_ref, *,
                       G, D, tq, tk, nk, scale, lam_init):
    j, qi, ki = pl.program_id(1), pl.program_id(2), pl.program_id(3)
    DV = 2 * D

    @pl.when(ki == 0)
    def _():
        m_ref[...] = jnp.full(m_ref.shape, NEG, F32)
        l_ref[...] = jnp.zeros(l_ref.shape, F32)
        acc_ref[...] = jnp.zeros(acc_ref.shape, F32)

    @pl.when(ki <= qi)
    def _():
        v_b = v_ref[...].astype(BF16)
        mask = _causal_mask(qi, ki, tq, tk)
        kpos = (lax.broadcasted_iota(jnp.int32, (1, tk), 1) + (ki * tk - qi * tq)).astype(F32)
        for mp in range(2):
            k_b = k_ref[:, mp * D:(mp + 1) * D].astype(BF16)
            for g in range(G):
                slot = g * 2 + mp
                q_b = (q_ref[:, slot * D:(slot + 1) * D] * scale).astype(BF16)
                s = lax.dot_general(q_b, k_b, NT_DIMS, preferred_element_type=F32)
                s = jnp.where(mask, s + sl_ref[j * G + g] * kpos, NEG)
                _online_update(s, v_b, m_ref, l_ref, acc_ref, slot, slot * DV, DV)

    @pl.when(ki == nk - 1)
    def _():
        lam = _diff_lambda(lam_ref, lam_init)
        for g in range(G):
            s1, s2 = g * 2, g * 2 + 1
            o = (acc_ref[:, s1 * DV:(s1 + 1) * DV] / l_ref[s1]
                 - lam * (acc_ref[:, s2 * DV:(s2 + 1) * DV] / l_ref[s2]))
            o = _rms(o, g_ref[...]) * (1.0 - lam_init)
            o_ref[:, g * DV:(g + 1) * DV] = o.astype(o_ref.dtype)


def _diff_flash(z, slopes, lam_rows, subln, B, S, *, H, KV, D, lam_init, t=512):
    G = H // KV
    DV = 2 * D
    t = min(t, S)
    nq = S // t
    kern = functools.partial(_diff_flash_kernel, G=G, D=D, tq=t, tk=t, nk=nq, scale=D ** -0.5, lam_init=lam_init)
    return pl.pallas_call(
        kern,
        grid=(B, KV, nq, nq),
        in_specs=[
            pl.BlockSpec(memory_space=pltpu.SMEM),
            pl.BlockSpec((t, G * DV), lambda b, j, qi, ki: (b * nq + qi, j)),
            pl.BlockSpec((t, DV), lambda b, j, qi, ki: (b * nq + jnp.minimum(ki, qi), H + j)),
            pl.BlockSpec((t, DV), lambda b, j, qi, ki: (b * nq + jnp.minimum(ki, qi), H + KV + j)),
            pl.BlockSpec((4, D), lambda b, j, qi, ki: (0, 0)),
            pl.BlockSpec((1, DV), lambda b, j, qi, ki: (0, 0)),
        ],
        out_specs=pl.BlockSpec((t, G * DV), lambda b, j, qi, ki: (b * nq + qi, j)),
        out_shape=jax.ShapeDtypeStruct((B * S, H * DV), BF16),
        scratch_shapes=[pltpu.VMEM((2 * G, t, 1), F32), pltpu.VMEM((2 * G, t, 1), F32),
                        pltpu.VMEM((t, 2 * G * DV), F32)],
        compiler_params=_params("arbitrary", "arbitrary", "arbitrary", "arbitrary"),
    )(slopes, z, z, z, lam_rows, subln.reshape(1, DV))


def _swa_flash_kernel(sl_ref, sk_ref, q_ref, kc_ref, kp_ref, vc_ref, vp_ref, o_ref, *, H, KV, W, scale):
    n = pl.program_id(1)
    G = H // KV
    half = LANES // 2
    lane = lax.broadcasted_iota(jnp.int32, (1, LANES), 1)
    lo = lane < half
    i = lax.broadcasted_iota(jnp.int32, (W, 2 * W), 0)
    jj = lax.broadcasted_iota(jnp.int32, (W, 2 * W), 1)
    rel = W + i - jj
    mask = (rel >= 0) & (rel <= W) & ((jj >= W) | (n > 0))
    relf = rel.astype(F32)

    def both_halves(x, take_lo):
        y = jnp.where(lo if take_lo else ~lo, x, 0.0)
        return y + pltpu.roll(y, half, 1)

    for pair in range(KV // 2):
        cs = slice(pair * LANES, (pair + 1) * LANES)
        kk = jnp.concatenate([kp_ref[:, cs], kc_ref[:, cs]], axis=0)
        vv = jnp.concatenate([vp_ref[:, cs], vc_ref[:, cs]], axis=0)
        for e in range(2):
            j = 2 * pair + e
            k_b = both_halves(kk, e == 0).astype(BF16)
            v_both = both_halves(vv, e == 0)
            v_lo = jnp.where(lo, v_both, 0.0).astype(BF16)
            v_hi = jnp.where(lo, 0.0, v_both).astype(BF16)
            for hp in range(G // 2):
                cb = j * (G // 2) + hp
                qp = q_ref[:, cb * LANES:(cb + 1) * LANES] * scale
                acc = jnp.zeros((W, LANES), F32)
                for hh in range(2):
                    h = 2 * cb + hh
                    q_b = jnp.where(lo if hh == 0 else ~lo, qp, 0.0).astype(BF16)
                    s = lax.dot_general(q_b, k_b, NT_DIMS, preferred_element_type=F32)
                    s = jnp.where(mask, s - sl_ref[h] * relf, NEG)
                    snk = sk_ref[h]
                    m = jnp.maximum(jnp.max(s, axis=1, keepdims=True), snk)
                    p = jnp.exp(s - m)
                    den = jnp.sum(p, axis=1, keepdims=True) + jnp.exp(snk - m)
                    p = (p / den).astype(BF16)
                    acc = acc + jnp.dot(p, v_lo if hh == 0 else v_hi, preferred_element_type=F32)
                o_ref[:, cb * LANES:(cb + 1) * LANES] = acc.astype(o_ref.dtype)


def _swa_flash(z, slopes, sinks, B, S, *, H, KV, D, W):
    assert D * 2 == LANES and W == LANES and KV % 2 == 0 and (H // KV) % 2 == 0
    nb = S // W
    qw, kw = H * D, KV * D
    kblk = qw // kw
    kern = functools.partial(_swa_flash_kernel, H=H, KV=KV, W=W, scale=D ** -0.5)
    smem = pl.BlockSpec(memory_space=pltpu.SMEM)
    return pl.pallas_call(
        kern,
        grid=(B, nb),
        in_specs=[
            smem, smem,
            pl.BlockSpec((W, qw), lambda b, n: (b * nb + n, 0)),
            pl.BlockSpec((W, kw), lambda b, n: (b * nb + n, kblk)),
            pl.BlockSpec((W, kw), lambda b, n: (b * nb + jnp.maximum(n - 1, 0), kblk)),
            pl.BlockSpec((W, kw), lambda b, n: (b * nb + n, kblk + 1)),
            pl.BlockSpec((W, kw), lambda b, n: (b * nb + jnp.maximum(n - 1, 0), kblk + 1)),
        ],
        out_specs=pl.BlockSpec((W, qw), lambda b, n: (b * nb + n, 0)),
        out_shape=jax.ShapeDtypeStruct((B * S, qw), BF16),
        compiler_params=_params("arbitrary", "arbitrary"),
    )(slopes, sinks, z, z, z, z, z)


def _cross_kernel(x_ref, g_ref, wq_ref, mk_ref, mv_ref, wo_ref, o_ref, wqb_ref, wob_ref, *, CH, D, scale):
    @pl.when((pl.program_id(0) == 0) & (pl.program_id(1) == 0))
    def _():
        wqb_ref[...] = wq_ref[...].astype(BF16)
        wob_ref[...] = wo_ref[...].astype(BF16)

    x = x_ref[...]
    xn = _rms(x, g_ref[...]).astype(BF16)
    q = jnp.dot(xn, wqb_ref[...], preferred_element_type=F32)
    outs = []
    for n in range(CH):
        cs = slice(n * D, (n + 1) * D)
        q_b = (q[:, cs] * scale).astype(BF16)
        s = lax.dot_general(q_b, mk_ref[:, cs].astype(BF16), NT_DIMS, preferred_element_type=F32)
        m = jnp.max(s, axis=1, keepdims=True)
        p = jnp.exp(s - m)
        p = (p / jnp.sum(p, axis=1, keepdims=True)).astype(BF16)
        outs.append(jnp.dot(p, mv_ref[:, cs].astype(BF16), preferred_element_type=F32).astype(BF16))
    o = jnp.concatenate(outs, axis=1)
    o_ref[...] = x + jnp.dot(o, wob_ref[...], preferred_element_type=F32)


def _cross_prompt(x, gain, w_q, w_o, layer, mkv, B, S, *, CH, D, tm=256):
    dm = x.shape[1]
    n_mem = mkv.shape[0] // B
    tm = min(tm, S)
    ns = S // tm
    kern = functools.partial(_cross_kernel, CH=CH, D=D, scale=D ** -0.5)
    return pl.pallas_call(
        kern,
        grid=(B, ns),
        in_specs=[
            pl.BlockSpec((tm, dm), lambda b, i: (b * ns + i, 0)),
            pl.BlockSpec((1, dm), lambda b, i: (0, 0)),
            pl.BlockSpec((None, dm, CH * D), lambda b, i: (layer, 0, 0)),
            pl.BlockSpec((n_mem, CH * D), lambda b, i: (b, 0)),
            pl.BlockSpec((n_mem, CH * D), lambda b, i: (b, 1)),
            pl.BlockSpec((None, CH * D, dm), lambda b, i: (layer, 0, 0)),
        ],
        out_specs=pl.BlockSpec((tm, dm), lambda b, i: (b * ns + i, 0)),
        out_shape=jax.ShapeDtypeStruct((B * S, dm), F32),
        scratch_shapes=[pltpu.VMEM((dm, CH * D), BF16), pltpu.VMEM((CH * D, dm), BF16)],
        compiler_params=_params("arbitrary", "arbitrary"),
    )(x, gain.reshape(1, dm), w_q, mkv, mkv, w_o)


def _strided_rows(ref, start, stride):
    return ref[pl.ds(start, LANES, stride=stride), :]


def _select_rows(tiles, row_group):
    out = tiles[0]
    for t in range(1, len(tiles)):
        out = jnp.where(row_group == t, tiles[t], out)
    return out


def _decode_update(s_all, v_tiles, row_kv, m_ref, l_ref, acc_refs):
    m_prev = m_ref[...]
    m_new = jnp.maximum(m_prev, jnp.max(s_all, axis=1, keepdims=True))
    alpha = jnp.exp(m_prev - m_new)
    p = jnp.exp(s_all - m_new)
    l_ref[...] = alpha * l_ref[...] + jnp.sum(p, axis=1, keepdims=True)
    p_b = p.astype(BF16)
    for part, acc_ref in enumerate(acc_refs):
        o = _select_rows([jnp.dot(p_b, vt, preferred_element_type=F32) for vt in v_tiles[part]], row_kv)
        acc_ref[...] = alpha * acc_ref[...] + o
    m_ref[...] = m_new


def _fox_decode_kernel(pt_ref, q_ref, kx_ref, vx_ref, zg_ref, bf_ref, *rest, P, KV, G, nc, scale):
    k_refs, v_refs, lf_refs = rest[:P], rest[P:2 * P], rest[2 * P:3 * P]
    o_ref, lfo_ref, m_ref, l_ref, run_ref, acc_ref = rest[3 * P:]
    c = pl.program_id(1)
    R = KV * G
    q = q_ref[0] * scale

    @pl.when(c == 0)
    def _():
        lf_new = _log_sigmoid(zg_ref[0] + bf_ref[...])
        lfo_ref[0] = lf_new
        run_ref[...] = lf_new
        m_ref[...] = jnp.sum(q * kx_ref[0], axis=1, keepdims=True)
        l_ref[...] = jnp.ones(l_ref.shape, F32)
        acc_ref[...] = vx_ref[0]

    q_b = q.astype(BF16)
    row_kv = lax.broadcasted_iota(jnp.int32, (R, LANES), 0) // G
    lfs = [lf_refs[i][...] for i in range(P)]
    parts = [_split3(lf) for lf in lfs]
    stacked = jnp.concatenate([parts[i][t] for t in range(3) for i in range(P)], axis=0)
    r = lax.broadcasted_iota(jnp.int32, (LANES, LANES), 0)
    cc = lax.broadcasted_iota(jnp.int32, (LANES, LANES), 1)
    later = (r > cc).astype(BF16)
    y = jnp.dot(stacked, later, preferred_element_type=F32)
    run = run_ref[...]
    s_list = []
    for i in range(P):
        w = y[i * R:(i + 1) * R] + y[(P + i) * R:(P + i + 1) * R] + y[(2 * P + i) * R:(2 * P + i + 1) * R]
        tiles = [lax.dot_general(q_b, _strided_rows(k_refs[i], j, KV).astype(BF16), NT_DIMS,
                                 preferred_element_type=F32) for j in range(KV)]
        s_list.append(_select_rows(tiles, row_kv) + (run + w))
        run = run + (w[:, 0:1] + lfs[i][:, 0:1])
    run_ref[...] = run
    v_tiles = [[jnp.concatenate([_strided_rows(v_refs[i], j, KV) for i in range(P)], axis=0).astype(BF16)
                for j in range(KV)]]
    _decode_update(jnp.concatenate(s_list, axis=1), v_tiles, row_kv, m_ref, l_ref, [acc_ref])

    @pl.when(c == nc - 1)
    def _():
        o_ref[0] = acc_ref[...] / l_ref[...]


def _fox_decode(q, kx, vx, zg, b_f, k_pages, v_pages, lf_pages, page_table, layer, *, KV, G, D):
    NB, n_pages = page_table.shape
    H = KV * G
    P = min(PAGES_PER_STEP, n_pages)
    assert n_pages % P == 0 and k_pages.shape[2] == LANES * KV
    nc = n_pages // P

    def page_map(i):
        return lambda b, c, pt: (layer, pt[b, n_pages - 1 - (c * P + i)], 0, 0)

    head = lambda b, c, pt: (b, 0, 0)
    k_specs = [pl.BlockSpec((None, None, LANES * KV, D), page_map(i)) for i in range(P)]
    lf_specs = [pl.BlockSpec((None, None, H, LANES), page_map(i)) for i in range(P)]
    kern = functools.partial(_fox_decode_kernel, P=P, KV=KV, G=G, nc=nc, scale=D ** -0.5)
    return pl.pallas_call(
        kern,
        grid_spec=pltpu.PrefetchScalarGridSpec(
            num_scalar_prefetch=1,
            grid=(NB, nc),
            in_specs=[pl.BlockSpec((1, H, D), head)] * 3
            + [pl.BlockSpec((1, H, 1), head), pl.BlockSpec((H, 1), lambda b, c, pt: (0, 0))]
            + k_specs + k_specs + lf_specs,
            out_specs=[pl.BlockSpec((1, H, D), head), pl.BlockSpec((1, H, 1), head)],
            scratch_shapes=[pltpu.VMEM((H, 1), F32)] * 3 + [pltpu.VMEM((H, D), F32)],
        ),
        out_shape=[jax.ShapeDtypeStruct((NB, H, D), F32), jax.ShapeDtypeStruct((NB, H, 1), F32)],
        compiler_params=_params("arbitrary", "arbitrary"),
    )(page_table, q, kx, vx, zg, b_f.reshape(H, 1), *([k_pages] * P), *([v_pages] * P), *([lf_pages] * P))


def _diff_decode_kernel(pt_ref, q_ref, kx_ref, vx_ref, sl_ref, lam_ref, g_ref, *rest,
                        P, KV, G, nc, scale, past, lam_init):
    k_refs, v_refs = rest[:P], rest[P:2 * P]
    o_ref, m_ref, l_ref, acc_lo_ref, acc_hi_ref = rest[2 * P:]
    c = pl.program_id(1)
    H = KV * G
    R = 2 * H
    q = q_ref[0] * scale

    @pl.when(c == 0)
    def _():
        m_ref[...] = jnp.sum(q * kx_ref[0], axis=1, keepdims=True)
        l_ref[...] = jnp.ones(l_ref.shape, F32)
        acc_lo_ref[...] = vx_ref[0, :, 0:LANES]
        acc_hi_ref[...] = vx_ref[0, :, LANES:2 * LANES]

    q_b = q.astype(BF16)
    row = lax.broadcasted_iota(jnp.int32, (R, LANES), 0)
    row_kv = (row % H) // G
    row_tile = (row // H) * KV + row_kv
    lane = lax.broadcasted_iota(jnp.int32, (1, LANES), 1)
    s_list = []
    for i in range(P):
        tiles = [lax.dot_general(q_b, _strided_rows(k_refs[i], t, 2 * KV).astype(BF16), NT_DIMS,
                                 preferred_element_type=F32) for t in range(2 * KV)]
        rel = (past - ((c * P + i) * LANES + lane)).astype(F32)
        s_list.append(_select_rows(tiles, row_tile) - sl_ref[...] * rel)
    v_tiles = [[jnp.concatenate([_strided_rows(v_refs[i], part * KV + j, 2 * KV) for i in range(P)],
                                axis=0).astype(BF16) for j in range(KV)] for part in range(2)]
    _decode_update(jnp.concatenate(s_list, axis=1), v_tiles, row_kv, m_ref, l_ref, [acc_lo_ref, acc_hi_ref])

    @pl.when(c == nc - 1)
    def _():
        lam = _diff_lambda(lam_ref, lam_init)
        inv = 1.0 / l_ref[...]
        lo = acc_lo_ref[...] * inv
        hi = acc_hi_ref[...] * inv
        o = jnp.concatenate([lo[0:H] - lam * lo[H:R], hi[0:H] - lam * hi[H:R]], axis=1)
        o_ref[0] = _rms(o, g_ref[...]) * (1.0 - lam_init)


def _diff_decode(q, kx, vx, slopes_rows, lam_rows, subln, k_pages, v_pages, page_table, layer, *, KV, G, D,
                 lam_init):
    NB, n_pages = page_table.shape
    H = KV * G
    R = 2 * H
    P = min(PAGES_PER_STEP, n_pages)
    assert n_pages % P == 0 and k_pages.shape[2] == LANES * 2 * KV
    nc = n_pages // P

    def page_map(i):
        return lambda b, c, pt: (layer, pt[b, c * P + i], 0, 0)

    head = lambda b, c, pt: (b, 0, 0)
    const = lambda b, c, pt: (0, 0)
    k_specs = [pl.BlockSpec((None, None, LANES * 2 * KV, D), page_map(i)) for i in range(P)]
    kern = functools.partial(_diff_decode_kernel, P=P, KV=KV, G=G, nc=nc, scale=D ** -0.5,
                             past=n_pages * LANES, lam_init=lam_init)
    return pl.pallas_call(
        kern,
        grid_spec=pltpu.PrefetchScalarGridSpec(
            num_scalar_prefetch=1,
            grid=(NB, nc),
            in_specs=[pl.BlockSpec((1, R, D), head), pl.BlockSpec((1, R, D), head), pl.BlockSpec((1, R, 2 * D), head),
                      pl.BlockSpec((R, 1), const), pl.BlockSpec((4, D), const), pl.BlockSpec((1, 2 * D), const)]
            + k_specs + k_specs,
            out_specs=pl.BlockSpec((1, H, 2 * D), head),
            scratch_shapes=[pltpu.VMEM((R, 1), F32)] * 2 + [pltpu.VMEM((R, D), F32)] * 2,
        ),
        out_shape=jax.ShapeDtypeStruct((NB, H, 2 * D), F32),
        compiler_params=_params("arbitrary", "arbitrary"),
    )(page_table, q, kx, vx, slopes_rows, lam_rows, subln.reshape(1, 2 * D), *([k_pages] * P), *([v_pages] * P))


def _swa_decode_kernel(q_ref, kt_ref, vt_ref, kx_ref, vx_ref, sl_ref, sk_ref, o_ref, *, H, KV, D, W, scale):
    G = H // KV
    q = q_ref[0] * scale
    s = jnp.dot(q.astype(BF16), kt_ref[0].astype(BF16), preferred_element_type=F32)
    rel = (W - lax.broadcasted_iota(jnp.int32, (1, W), 1)).astype(F32)
    s = s - sl_ref[...] * rel
    s_new = jnp.sum(q * kx_ref[0], axis=1, keepdims=True)
    snk = sk_ref[...]
    m = jnp.maximum(jnp.maximum(jnp.max(s, axis=1, keepdims=True), s_new), snk)
    p = jnp.exp(s - m)
    p_new = jnp.exp(s_new - m)
    den = jnp.sum(p, axis=1, keepdims=True) + p_new + jnp.exp(snk - m)
    o = lax.dot_general((p / den).astype(BF16), vt_ref[0].astype(BF16), NT_DIMS, preferred_element_type=F32)
    o = o + (p_new / den) * vx_ref[0]
    col_kv = lax.broadcasted_iota(jnp.int32, (H, KV * D), 1) // D
    row_kv = lax.broadcasted_iota(jnp.int32, (H, KV * D), 0) // G
    o = jnp.where(col_kv == row_kv, o, 0.0)
    folded = o[:, 0:LANES]
    for t in range(1, KV * D // LANES):
        folded = folded + o[:, t * LANES:(t + 1) * LANES]
    odd = (lax.broadcasted_iota(jnp.int32, (H, LANES), 0) // G) % 2 == 1
    o_ref[0] = jnp.where(odd, pltpu.roll(folded, LANES // 2, 1), folded)


def _swa_decode(q_exp, k_t, v_t, kx, vx, slopes, sinks, *, H, KV, D, W):
    NB = q_exp.shape[0]
    kern = functools.partial(_swa_decode_kernel, H=H, KV=KV, D=D, W=W, scale=D ** -0.5)
    row = lambda b: (b, 0, 0)
    return pl.pallas_call(
        kern,
        grid=(NB,),
        in_specs=[pl.BlockSpec((1, H, KV * D), row), pl.BlockSpec((1, KV * D, W), row),
                  pl.BlockSpec((1, KV * D, W), row), pl.BlockSpec((1, H, KV * D), row),
                  pl.BlockSpec((1, H, KV * D), row),
                  pl.BlockSpec((H, 1), lambda b: (0, 0)), pl.BlockSpec((H, 1), lambda b: (0, 0))],
        out_specs=pl.BlockSpec((1, H, LANES), row),
        out_shape=jax.ShapeDtypeStruct((NB, H, LANES), F32),
        compiler_params=_params("arbitrary"),
    )(q_exp, k_t, v_t, kx, vx, slopes.reshape(H, 1), sinks.reshape(H, 1))


def _cross_decode_kernel(q_ref, k_ref, v_ref, o_ref, *, CH, scale):
    q_b = (q_ref[0] * scale).astype(BF16)
    s = lax.dot_general(q_b, k_ref[0].astype(BF16), NT_DIMS, preferred_element_type=F32)
    row = lax.broadcasted_iota(jnp.int32, s.shape, 0)
    col = lax.broadcasted_iota(jnp.int32, s.shape, 1)
    s = jnp.where(col % CH == row, s, NEG)
    m = jnp.max(s, axis=1, keepdims=True)
    p = jnp.where(col % CH == row, jnp.exp(s - m), 0.0)
    den = jnp.maximum(jnp.sum(p, axis=1, keepdims=True), 1e-30)
    o = jnp.dot((p / den).astype(BF16), v_ref[0].astype(BF16), preferred_element_type=F32)
    o_ref[0] = o[0:CH]


def _cross_decode(q_pad, mem_k, mem_v, layer, *, CH, D):
    NB, R, _ = q_pad.shape
    n = mem_k.shape[2]
    kern = functools.partial(_cross_decode_kernel, CH=CH, scale=D ** -0.5)
    return pl.pallas_call(
        kern,
        grid=(NB,),
        in_specs=[pl.BlockSpec((1, R, D), lambda b: (b, 0, 0)),
                  pl.BlockSpec((None, 1, n, D), lambda b: (layer, b, 0, 0)),
                  pl.BlockSpec((None, 1, n, D), lambda b: (layer, b, 0, 0))],
        out_specs=pl.BlockSpec((1, CH, D), lambda b: (b, 0, 0)),
        out_shape=jax.ShapeDtypeStruct((NB, CH, D), F32),
        compiler_params=_params("arbitrary"),
    )(q_pad, mem_k, mem_v)


def _alibi_slopes(n):
    return jnp.exp2(-8.0 * jnp.arange(1, n + 1, dtype=F32) / n)


def kernel(x_prompt, x_sample, mem_prompt, cache_fox_k, cache_fox_v, cache_fox_logf, cache_diff_k, cache_diff_v, state_swa_k, state_swa_v, cache_mem_k, cache_mem_v, page_table, norm_mix, w_a_in, b_a_f, w_a_out, w_b_in, lam_q1, lam_k1, lam_q2, lam_k2, subln_b, w_b_out, w_c_in, b_c_in, sinks_c, w_c_out, norm_cross, norm_mem, w_x_q, w_x_kv, w_x_out, norm_ffn, w_ff_gu, w_ff_down, norm_final):
    B, S, dm = x_prompt.shape
    NB = x_sample.shape[0]
    depth = norm_mix.shape[0]
    n_mem = mem_prompt.shape[1]
    _, n_pool, page, KVA, HDA = cache_fox_k.shape
    HA = cache_fox_logf.shape[-1]
    KVB, HDB = cache_diff_k.shape[3], cache_diff_k.shape[4] // 2
    HB = dm // (2 * HDB)
    _, _, W, KVC, HDC = state_swa_k.shape
    HC = dm // HDC
    CH, CHD = cache_mem_k.shape[3], cache_mem_k.shape[4]
    assert page == LANES and x_sample.shape[1] == 1

    xp = x_prompt.reshape(B * S, dm)
    xs = x_sample.reshape(NB, dm)
    mem = mem_prompt.reshape(B * n_mem, dm)

    fox_k_pages = cache_fox_k.reshape(-1, n_pool, page * KVA, HDA)
    fox_v_pages = cache_fox_v.reshape(-1, n_pool, page * KVA, HDA)
    fox_lf_pages = jnp.swapaxes(cache_fox_logf, 2, 3)

    def diff_pages(c):
        c = c.reshape(-1, n_pool, page, KVB, 2, HDB)
        return jnp.transpose(c, (0, 1, 2, 4, 3, 5)).reshape(-1, n_pool, page * 2 * KVB, HDB)

    diff_k_pages, diff_v_pages = diff_pages(cache_diff_k), diff_pages(cache_diff_v)
    mem_k_rows = cache_mem_k.reshape(depth, NB, n_mem * CH, CHD)
    mem_v_rows = cache_mem_v.reshape(depth, NB, n_mem * CH, CHD)

    slopes_b = _alibi_slopes(HB)
    slopes_c = _alibi_slopes(HC)

    pfk, pfv, pfl, sfk, sfv, sfl = [], [], [], [], [], []
    pdk, pdv, sdk, sdv = [], [], [], []
    pck, pcv, sck, scv = [], [], [], []
    pmk, pmv = [], []
    ia = ib = ic = 0
    for i in range(depth):
        kind = i % 3
        if kind == 0:
            nq, nkv = HA * HDA, KVA * HDA
            G = HA // KVA
            z = _mm(xp, w_a_in, ia, gain=norm_mix[i])
            zs = _mm(xs, w_a_in, ia, gain=norm_mix[i])
            zg_t = jnp.swapaxes(z[:, nq + 2 * nkv:].reshape(B, S, HA), 1, 2)
            lf_t, c_t = _fox_gate(zg_t, b_a_f[ia])
            o = _fox_flash(z, c_t.reshape(B, KVA, G, S), B, S, H=HA, KV=KVA, D=HDA)
            xp = _mm(o, w_a_out, ia, residual=xp)
            pfk.append(z[:, nq:nq + nkv].reshape(B, S, KVA, HDA))
            pfv.append(z[:, nq + nkv:nq + 2 * nkv].reshape(B, S, KVA, HDA))
            pfl.append(jnp.swapaxes(lf_t, 1, 2))

            ks = zs[:, nq:nq + nkv].reshape(NB, KVA, HDA)
            vs = zs[:, nq + nkv:nq + 2 * nkv].reshape(NB, KVA, HDA)
            os_, lfs = _fox_decode(
                zs[:, :nq].reshape(NB, HA, HDA), jnp.repeat(ks, G, axis=1), jnp.repeat(vs, G, axis=1),
                zs[:, nq + 2 * nkv:].reshape(NB, HA, 1), b_a_f[ia],
                fox_k_pages, fox_v_pages, fox_lf_pages, page_table, ia, KV=KVA, G=G, D=HDA)
            xs = _mm(os_.reshape(NB, nq), w_a_out, ia, residual=xs)
            sfk.append(ks.reshape(NB, 1, KVA, HDA))
            sfv.append(vs.reshape(NB, 1, KVA, HDA))
            sfl.append(lfs.reshape(NB, 1, HA))
            ia += 1
        elif kind == 1:
            lam_init = 0.8 - 0.6 * math.exp(-0.3 * i)
            nq, nkv = HB * 2 * HDB, KVB * 2 * HDB
            G = HB // KVB
            lam_rows = jnp.stack([lam_q1[ib], lam_k1[ib], lam_q2[ib], lam_k2[ib]])
            z = _mm(xp, w_b_in, ib, gain=norm_mix[i])
            zs = _mm(xs, w_b_in, ib, gain=norm_mix[i])
            o = _diff_flash(z, slopes_b, lam_rows, subln_b[ib], B, S, H=HB, KV=KVB, D=HDB, lam_init=lam_init)
            xp = _mm(o, w_b_out, ib, residual=xp)
            pdk.append(z[:, nq:nq + nkv].reshape(B, S, KVB, 2 * HDB))
            pdv.append(z[:, nq + nkv:].reshape(B, S, KVB, 2 * HDB))

            ks = zs[:, nq:nq + nkv].reshape(NB, KVB, 2, HDB)
            vs = zs[:, nq + nkv:].reshape(NB, KVB, 2 * HDB)
            q_rows = jnp.transpose(zs[:, :nq].reshape(NB, HB, 2, HDB), (0, 2, 1, 3)).reshape(NB, 2 * HB, HDB)
            kx = jnp.transpose(jnp.repeat(ks, G, axis=1), (0, 2, 1, 3)).reshape(NB, 2 * HB, HDB)
            vx = jnp.tile(jnp.repeat(vs, G, axis=1), (1, 2, 1))
            os_ = _diff_decode(q_rows, kx, vx, jnp.tile(slopes_b, 2).reshape(2 * HB, 1), lam_rows, subln_b[ib],
                               diff_k_pages, diff_v_pages, page_table, ib, KV=KVB, G=G, D=HDB, lam_init=lam_init)
            xs = _mm(os_.reshape(NB, nq), w_b_out, ib, residual=xs)
            sdk.append(ks.reshape(NB, 1, KVB, 2 * HDB))
            sdv.append(vs.reshape(NB, 1, KVB, 2 * HDB))
            ib += 1
        else:
            nq, nkv = HC * HDC, KVC * HDC
            G = HC // KVC
            z = _mm(xp, w_c_in, ic, gain=norm_mix[i], bias=b_c_in[ic])
            zs = _mm(xs, w_c_in, ic, gain=norm_mix[i], bias=b_c_in[ic])
            o = _swa_flash(z, slopes_c, sinks_c[ic], B, S, H=HC, KV=KVC, D=HDC, W=W)
            xp = _mm(o, w_c_out, ic, residual=xp)
            z3 = z.reshape(B, S, nq + 2 * nkv)
            pck.append(z3[:, S - W:, nq:nq + nkv].reshape(B, W, KVC, HDC))
            pcv.append(z3[:, S - W:, nq + nkv:].reshape(B, W, KVC, HDC))

            ks = zs[:, nq:nq + nkv].reshape(NB, 1, KVC, HDC)
            vs = zs[:, nq + nkv:].reshape(NB, 1, KVC, HDC)
            blk = (jnp.arange(HC)[:, None] // G == jnp.arange(KVC)[None, :]).astype(F32)
            q_exp = (zs[:, :nq].reshape(NB, HC, 1, HDC) * blk[None, :, :, None]).reshape(NB, HC, nkv)
            kx = jnp.broadcast_to(zs[:, None, nq:nq + nkv], (NB, HC, nkv))
            vx = jnp.broadcast_to(zs[:, None, nq + nkv:], (NB, HC, nkv))
            k_t = jnp.transpose(state_swa_k[ic], (0, 2, 3, 1)).reshape(NB, nkv, W)
            v_t = jnp.transpose(state_swa_v[ic], (0, 2, 3, 1)).reshape(NB, nkv, W)
            o2 = _swa_decode(q_exp, k_t, v_t, kx, vx, slopes_c, sinks_c[ic], H=HC, KV=KVC, D=HDC, W=W)
            xs = _mm(o2[:, :, :HDC].reshape(NB, nq), w_c_out, ic, residual=xs)
            sck.append(jnp.concatenate([state_swa_k[ic][:, 1:], ks], axis=1))
            scv.append(jnp.concatenate([state_swa_v[ic][:, 1:], vs], axis=1))
            ic += 1

        mkv = _mm(mem, w_x_kv, i, gain=norm_mem[i])
        xp = _cross_prompt(xp, norm_cross[i], w_x_q, w_x_out, i, mkv, B, S, CH=CH, D=CHD)
        pmk.append(mkv[:, :CH * CHD].reshape(B, n_mem, CH, CHD))
        pmv.append(mkv[:, CH * CHD:].reshape(B, n_mem, CH, CHD))
        qs = _mm(xs, w_x_q, i, gain=norm_cross[i]).reshape(NB, CH, CHD)
        q_pad = jnp.pad(qs, ((0, 0), (0, 16 - CH), (0, 0)))
        oc = _cross_decode(q_pad, mem_k_rows, mem_v_rows, i, CH=CH, D=CHD)
        xs = _mm(oc.reshape(NB, CH * CHD), w_x_out, i, residual=xs)

        h = _mm(xp, w_ff_gu, i, gain=norm_ffn[i], glu=True, out_dtype=BF16)
        xp = _mm(h, w_ff_down, i, residual=xp, tn=256)
        hs = _mm(xs, w_ff_gu, i, gain=norm_ffn[i], glu=True, out_dtype=BF16)
        xs = _mm(hs, w_ff_down, i, residual=xs, tn=256)

    y_prompt = _rmsnorm(xp, norm_final).reshape(B, S, dm)
    y_sample = _rmsnorm(xs, norm_final).reshape(NB, 1, dm)
    return (y_prompt, y_sample,
            jnp.stack(pfk), jnp.stack(pfv), jnp.stack(pfl),
            jnp.stack(pdk), jnp.stack(pdv),
            jnp.stack(pck), jnp.stack(pcv),
            jnp.stack(pmk), jnp.stack(pmv),
            jnp.stack(sfk), jnp.stack(sfv), jnp.stack(sfl),
            jnp.stack(sdk), jnp.stack(sdv),
            jnp.stack(sck), jnp.stack(scv))
```

```python
import functools
import math

import jax
import jax.numpy as jnp
from jax import lax
from jax.experimental import pallas as pl
from jax.experimental.pallas import tpu as pltpu

F32 = jnp.float32
BF16 = jnp.bfloat16
EPS = 1e-6
NEG = -1e30
LOG2E = math.log2(math.e)
LANES = 128
VMEM_LIMIT_BYTES = 56 * 1024 * 1024
FOX_PAGES_PER_STEP = 16
DIFF_PAGES_PER_STEP = 8
NT_DIMS = (((1,), (1,)), ((), ()))


def _params(*sem):
    return pltpu.CompilerParams(dimension_semantics=sem, vmem_limit_bytes=VMEM_LIMIT_BYTES)


def _rms(x, g):
    return x * lax.rsqrt(jnp.mean(x * x, axis=-1, keepdims=True) + EPS) * g


def _log_sigmoid(x):
    return jnp.minimum(x, 0.0) - jnp.log(1.0 + jnp.exp(-jnp.abs(x)))


def _split3(x):
    hi = x.astype(BF16)
    r1 = x - hi.astype(F32)
    mid = r1.astype(BF16)
    lo = (r1 - mid.astype(F32)).astype(BF16)
    return hi, mid, lo


def _mm_kernel(*refs, has_gain, has_bias, has_res, glu, stage_x, tm, rc):
    refs = list(refs)
    x_ref = refs.pop(0)
    g_ref = refs.pop(0) if has_gain else None
    w_ref = refs.pop(0)
    w2_ref = refs.pop(0) if glu else None
    b_ref = refs.pop(0) if has_bias else None
    r_ref = refs.pop(0) if has_res else None
    o_ref = refs.pop(0)
    xs_ref = refs.pop(0) if stage_x else None

    if stage_x:
        @pl.when(pl.program_id(1) == 0)
        def _():
            def body(c, carry):
                r0 = pl.multiple_of(c * rc, rc)
                xc = x_ref[pl.ds(r0, rc), :].astype(F32)
                if has_gain:
                    xc = _rms(xc, g_ref[...])
                xs_ref[pl.ds(r0, rc), :] = xc.astype(BF16)
                return carry

            lax.fori_loop(0, tm // rc, body, 0)

        xb = xs_ref[...]
    else:
        xb = x_ref[...]
    acc = jnp.dot(xb, w_ref[...].astype(BF16), preferred_element_type=F32)
    if glu:
        up = jnp.dot(xb, w2_ref[...].astype(BF16), preferred_element_type=F32)
        acc = acc * jax.nn.sigmoid(acc) * up
    if has_bias:
        acc = acc + b_ref[...]
    if has_res:
        acc = r_ref[...] + acc
    o_ref[...] = acc.astype(o_ref.dtype)


def _mm(x, w, layer, *, gain=None, bias=None, residual=None, glu=False, out_dtype=F32, tm=1024, tn=512):
    M, K = x.shape
    N = w.shape[-1] // (2 if glu else 1)
    tm = min(tm, M)
    assert M % tm == 0 and w.shape[-2] == K
    nj = pl.cdiv(N, tn)
    assert not glu or N % tn == 0
    stage_x = x.dtype != BF16 or gain is not None
    rc = min(tm, 128)
    in_specs = [pl.BlockSpec((tm, K), lambda i, j: (i, 0))]
    args = [x]
    if gain is not None:
        in_specs.append(pl.BlockSpec((1, K), lambda i, j: (0, 0)))
        args.append(gain.reshape(1, K).astype(F32))
    in_specs.append(pl.BlockSpec((None, K, tn), lambda i, j: (layer, 0, j)))
    args.append(w)
    if glu:
        in_specs.append(pl.BlockSpec((None, K, tn), lambda i, j: (layer, 0, j + nj)))
        args.append(w)
    if bias is not None:
        in_specs.append(pl.BlockSpec((1, tn), lambda i, j: (0, j)))
        args.append(bias.reshape(1, N).astype(F32))
    if residual is not None:
        in_specs.append(pl.BlockSpec((tm, tn), lambda i, j: (i, j)))
        args.append(residual)
    kern = functools.partial(_mm_kernel, has_gain=gain is not None, has_bias=bias is not None,
                             has_res=residual is not None, glu=glu, stage_x=stage_x, tm=tm, rc=rc)
    return pl.pallas_call(
        kern,
        grid=(M // tm, nj),
        in_specs=in_specs,
        out_specs=pl.BlockSpec((tm, tn), lambda i, j: (i, j)),
        out_shape=jax.ShapeDtypeStruct((M, N), out_dtype),
        scratch_shapes=[pltpu.VMEM((tm, K), BF16)] if stage_x else [],
        compiler_params=_params("arbitrary", "arbitrary"),
        name=f"mm{'_glu' if glu else ''}_{M}x{K}x{N}",
    )(*args)


def _rmsnorm_kernel(x_ref, g_ref, o_ref):
    o_ref[...] = _rms(x_ref[...], g_ref[...])


def _rmsnorm(x, g, tm=256):
    M, K = x.shape
    tm = min(tm, M)
    return pl.pallas_call(
        _rmsnorm_kernel,
        grid=(M // tm,),
        in_specs=[pl.BlockSpec((tm, K), lambda i: (i, 0)), pl.BlockSpec((1, K), lambda i: (0, 0))],
        out_specs=pl.BlockSpec((tm, K), lambda i: (i, 0)),
        out_shape=jax.ShapeDtypeStruct((M, K), F32),
        compiler_params=_params("arbitrary"),
        name="final_norm",
    )(x, g.reshape(1, K))


def _lanes(x, width):
    return x if width == LANES else pltpu.repeat(x, width // LANES, axis=1)


def _online_update(s, v_b, m_ref, l_ref, acc_ref, idx, col0, dv):
    m_prev = m_ref[idx]
    m_new = jnp.maximum(m_prev, jnp.max(s, axis=1, keepdims=True))
    alpha = jnp.exp2(m_prev - m_new)
    p = jnp.exp2(s - _lanes(m_new, s.shape[1]))
    l_ref[idx] = alpha * l_ref[idx] + jnp.sum(p, axis=1, keepdims=True)
    acc_ref[:, col0:col0 + dv] = _lanes(alpha, dv) * acc_ref[:, col0:col0 + dv] + jnp.dot(
        p.astype(BF16), v_b, preferred_element_type=F32)
    m_ref[idx] = m_new


def _diag_mask(t):
    row = lax.broadcasted_iota(jnp.int32, (t, t), 0)
    col = lax.broadcasted_iota(jnp.int32, (t, t), 1)
    return col <= row


def _fox_gate_kernel(zg_ref, b_ref, lf_ref, c_ref, *, S):
    r = lax.broadcasted_iota(jnp.int32, (LANES, LANES), 0)
    c_ = lax.broadcasted_iota(jnp.int32, (LANES, LANES), 1)
    tri = (r <= c_).astype(BF16)
    H = zg_ref.shape[0]
    carry = jnp.zeros((H, 1), F32)
    for t in range(S // LANES):
        sl = slice(t * LANES, (t + 1) * LANES)
        lf = _log_sigmoid(zg_ref[:, sl] + b_ref[...])
        lf_ref[:, sl] = lf
        hi, mid, lo = _split3(lf)
        y = jnp.dot(jnp.concatenate([hi, mid, lo], axis=0), tri, preferred_element_type=F32)
        c = y[0:H] + y[H:2 * H] + y[2 * H:3 * H] + carry
        c_ref[:, sl] = c
        carry = c[:, LANES - 1:LANES]


def _fox_gate(zg_t, b_f):
    B, H, S = zg_t.shape
    spec = pl.BlockSpec((None, H, S), lambda b: (b, 0, 0))
    return pl.pallas_call(
        functools.partial(_fox_gate_kernel, S=S),
        grid=(B,),
        in_specs=[spec, pl.BlockSpec((H, 1), lambda b: (0, 0))],
        out_specs=[spec, spec],
        out_shape=[jax.ShapeDtypeStruct((B, H, S), F32)] * 2,
        compiler_params=_params("arbitrary"),
        name="fox_gate",
    )(zg_t, b_f.reshape(H, 1))


def _flash_init(q_ref, qb_ref, m_ref, l_ref, acc_ref, scale):
    qb_ref[...] = (q_ref[...] * (scale * LOG2E)).astype(BF16)
    m_ref[...] = jnp.full(m_ref.shape, NEG, F32)
    l_ref[...] = jnp.zeros(l_ref.shape, F32)
    acc_ref[...] = jnp.zeros(acc_ref.shape, F32)


def _fox_flash_kernel(q_ref, k_ref, v_ref, ck_ref, o_ref, qb_ref, m_ref, l_ref, acc_ref, *, G, D, t, nk, scale):
    qi, ki = pl.program_id(2), pl.program_id(3)

    @pl.when(ki == 0)
    def _():
        _flash_init(q_ref, qb_ref, m_ref, l_ref, acc_ref, scale)

    def step(diagonal):
        k_b = k_ref[...].astype(BF16)
        v_b = v_ref[...].astype(BF16)
        for g in range(G):
            s = lax.dot_general(qb_ref[:, g * D:(g + 1) * D], k_b, NT_DIMS, preferred_element_type=F32)
            s = s - ck_ref[g:g + 1, :] * LOG2E
            if diagonal:
                s = jnp.where(_diag_mask(t), s, NEG)
            _online_update(s, v_b, m_ref, l_ref, acc_ref, g, g * D, D)

    @pl.when(ki < qi)
    def _():
        step(False)

    @pl.when(ki == qi)
    def _():
        step(True)

    @pl.when(ki == nk - 1)
    def _():
        for g in range(G):
            o_ref[:, g * D:(g + 1) * D] = (acc_ref[:, g * D:(g + 1) * D] / l_ref[g]).astype(o_ref.dtype)


def _fox_flash(z, c_t, B, S, *, H, KV, D, t=512):
    G = H // KV
    t = min(t, S)
    nq = S // t
    kern = functools.partial(_fox_flash_kernel, G=G, D=D, t=t, nk=nq, scale=D ** -0.5)
    return pl.pallas_call(
        kern,
        grid=(B, KV, nq, nq),
        in_specs=[
            pl.BlockSpec((t, G * D), lambda b, j, qi, ki: (b * nq + qi, j)),
            pl.BlockSpec((t, D), lambda b, j, qi, ki: (b * nq + jnp.minimum(ki, qi), H + j)),
            pl.BlockSpec((t, D), lambda b, j, qi, ki: (b * nq + jnp.minimum(ki, qi), H + KV + j)),
            pl.BlockSpec((None, None, G, t), lambda b, j, qi, ki: (b, j, 0, jnp.minimum(ki, qi))),
        ],
        out_specs=pl.BlockSpec((t, G * D), lambda b, j, qi, ki: (b * nq + qi, j)),
        out_shape=jax.ShapeDtypeStruct((B * S, H * D), BF16),
        scratch_shapes=[pltpu.VMEM((t, G * D), BF16), pltpu.VMEM((G, t, LANES), F32),
                        pltpu.VMEM((G, t, LANES), F32), pltpu.VMEM((t, G * D), F32)],
        compiler_params=_params("arbitrary", "arbitrary", "arbitrary", "arbitrary"),
        name="fox_flash",
    )(z, z, z, c_t)


def _diff_lambda(lam_ref, lam_init):
    a = jnp.sum(lam_ref[0:1, :] * lam_ref[1:2, :], axis=1, keepdims=True)
    b = jnp.sum(lam_ref[2:3, :] * lam_ref[3:4, :], axis=1, keepdims=True)
    return jnp.exp(a) - jnp.exp(b) + lam_init


def _diff_flash_kernel(sl_ref, q_ref, k_ref, v_ref, lam_ref, g_ref, o_ref, qb_ref, m_ref, l_ref, acc_ref, *,
                       G, D, t, nk, scale, lam_init):
    j, qi, ki = pl.program_id(1), pl.program_id(2), pl.program_id(3)
    DV = 2 * D

    @pl.when(ki == 0)
    def _():
        _flash_init(q_ref, qb_ref, m_ref, l_ref, acc_ref, scale)

    def step(diagonal):
        v_b = v_ref[...].astype(BF16)
        kpos = (lax.broadcasted_iota(jnp.int32, (1, t), 1) + (ki - qi) * t).astype(F32) * LOG2E
        for mp in range(2):
            k_b = k_ref[:, mp * D:(mp + 1) * D].astype(BF16)
            for g in range(G):
                slot = g * 2 + mp
                s = lax.dot_general(qb_ref[:, slot * D:(slot + 1) * D], k_b, NT_DIMS, preferred_element_type=F32)
                s = s + sl_ref[j * G + g] * kpos
                if diagonal:
                    s = jnp.where(_diag_mask(t), s, NEG)
                _online_update(s, v_b, m_ref, l_ref, acc_ref, slot, slot * DV, DV)

    @pl.when(ki < qi)
    def _():
        step(False)

    @pl.when(ki == qi)
    def _():
        step(True)

    @pl.when(ki == nk - 1)
    def _():
        lam = _diff_lambda(lam_ref, lam_init)
        for g in range(G):
            s1, s2 = g * 2, g * 2 + 1
            o = (acc_ref[:, s1 * DV:(s1 + 1) * DV] / _lanes(l_ref[s1], DV)
                 - lam * (acc_ref[:, s2 * DV:(s2 + 1) * DV] / _lanes(l_ref[s2], DV)))
            o = _rms(o, g_ref[...]) * (1.0 - lam_init)
            o_ref[:, g * DV:(g + 1) * DV] = o.astype(o_ref.dtype)


def _diff_flash(z, slopes, lam_rows, subln, B, S, *, H, KV, D, lam_init, t=512):
    G = H // KV
    DV = 2 * D
    t = min(t, S)
    nq = S // t
    kern = functools.partial(_diff_flash_kernel, G=G, D=D, t=t, nk=nq, scale=D ** -0.5, lam_init=lam_init)
    return pl.pallas_call(
        kern,
        grid=(B, KV, nq, nq),
        in_specs=[
            pl.BlockSpec(memory_space=pltpu.SMEM),
            pl.BlockSpec((t, G * DV), lambda b, j, qi, ki: (b * nq + qi, j)),
            pl.BlockSpec((t, DV), lambda b, j, qi, ki: (b * nq + jnp.minimum(ki, qi), H + j)),
            pl.BlockSpec((t, DV), lambda b, j, qi, ki: (b * nq + jnp.minimum(ki, qi), H + KV + j)),
            pl.BlockSpec((4, D), lambda b, j, qi, ki: (0, 0)),
            pl.BlockSpec((1, DV), lambda b, j, qi, ki: (0, 0)),
        ],
        out_specs=pl.BlockSpec((t, G * DV), lambda b, j, qi, ki: (b * nq + qi, j)),
        out_shape=jax.ShapeDtypeStruct((B * S, H * DV), BF16),
        scratch_shapes=[pltpu.VMEM((t, G * DV), BF16), pltpu.VMEM((2 * G, t, LANES), F32),
                        pltpu.VMEM((2 * G, t, LANES), F32), pltpu.VMEM((t, 2 * G * DV), F32)],
        compiler_params=_params("arbitrary", "arbitrary", "arbitrary", "arbitrary"),
        name="diff_flash",
    )(slopes, z, z, z, lam_rows, subln.reshape(1, DV))


def _swa_flash_kernel(sl_ref, sk_ref, q_ref, kc_ref, kp_ref, vc_ref, vp_ref, o_ref, *, H, KV, W, scale):
    n = pl.program_id(1)
    G = H // KV
    half = LANES // 2
    lane = lax.broadcasted_iota(jnp.int32, (1, LANES), 1)
    lo = lane < half
    i = lax.broadcasted_iota(jnp.int32, (W, 2 * W), 0)
    jj = lax.broadcasted_iota(jnp.int32, (W, 2 * W), 1)
    rel = W + i - jj
    mask = (rel >= 0) & (rel <= W) & ((jj >= W) | (n > 0))
    relf = rel.astype(F32)

    def both_halves(x, take_lo):
        y = jnp.where(lo if take_lo else ~lo, x, 0.0)
        return y + pltpu.roll(y, half, 1)

    for pair in range(KV // 2):
        cs = slice(pair * LANES, (pair + 1) * LANES)
        kk = jnp.concatenate([kp_ref[:, cs], kc_ref[:, cs]], axis=0)
        vv = jnp.concatenate([vp_ref[:, cs], vc_ref[:, cs]], axis=0)
        for e in range(2):
            j = 2 * pair + e
            k_b = both_halves(kk, e == 0).astype(BF16)
            v_both = both_halves(vv, e == 0)
            v_lo = jnp.where(lo, v_both, 0.0).astype(BF16)
            v_hi = jnp.where(lo, 0.0, v_both).astype(BF16)
            for hp in range(G // 2):
                cb = j * (G // 2) + hp
                qp = q_ref[:, cb * LANES:(cb + 1) * LANES] * scale
                acc = jnp.zeros((W, LANES), F32)
                for hh in range(2):
                    h = 2 * cb + hh
                    q_b = jnp.where(lo if hh == 0 else ~lo, qp, 0.0).astype(BF16)
                    s = lax.dot_general(q_b, k_b, NT_DIMS, preferred_element_type=F32)
                    s = jnp.where(mask, s - sl_ref[h] * relf, NEG)
                    snk = sk_ref[h]
                    m = jnp.maximum(jnp.max(s, axis=1, keepdims=True), snk)
                    p = jnp.exp(s - m)
                    den = jnp.sum(p, axis=1, keepdims=True) + jnp.exp(snk - m)
                    p = (p / den).astype(BF16)
                    acc = acc + jnp.dot(p, v_lo if hh == 0 else v_hi, preferred_element_type=F32)
                o_ref[:, cb * LANES:(cb + 1) * LANES] = acc.astype(o_ref.dtype)


def _swa_flash(z, slopes, sinks, B, S, *, H, KV, D, W):
    assert D * 2 == LANES and W == LANES and KV % 2 == 0 and (H // KV) % 2 == 0
    nb = S // W
    qw, kw = H * D, KV * D
    kblk = qw // kw
    kern = functools.partial(_swa_flash_kernel, H=H, KV=KV, W=W, scale=D ** -0.5)
    smem = pl.BlockSpec(memory_space=pltpu.SMEM)
    return pl.pallas_call(
        kern,
        grid=(B, nb),
        in_specs=[
            smem, smem,
            pl.BlockSpec((W, qw), lambda b, n: (b * nb + n, 0)),
            pl.BlockSpec((W, kw), lambda b, n: (b * nb + n, kblk)),
            pl.BlockSpec((W, kw), lambda b, n: (b * nb + jnp.maximum(n - 1, 0), kblk)),
            pl.BlockSpec((W, kw), lambda b, n: (b * nb + n, kblk + 1)),
            pl.BlockSpec((W, kw), lambda b, n: (b * nb + jnp.maximum(n - 1, 0), kblk + 1)),
        ],
        out_specs=pl.BlockSpec((W, qw), lambda b, n: (b * nb + n, 0)),
        out_shape=jax.ShapeDtypeStruct((B * S, qw), BF16),
        compiler_params=_params("arbitrary", "arbitrary"),
        name="swa_flash",
    )(slopes, sinks, z, z, z, z, z)


def _cross_kernel(x_ref, g_ref, wq_ref, mk_ref, mv_ref, wo_ref, o_ref, wqb_ref, wob_ref, *, CH, D, scale):
    @pl.when((pl.program_id(0) == 0) & (pl.program_id(1) == 0))
    def _():
        wqb_ref[...] = wq_ref[...].astype(BF16)
        wob_ref[...] = wo_ref[...].astype(BF16)

    x = x_ref[...]
    xn = _rms(x, g_ref[...]).astype(BF16)
    q = jnp.dot(xn, wqb_ref[...], preferred_element_type=F32)
    outs = []
    for n in range(CH):
        cs = slice(n * D, (n + 1) * D)
        q_b = (q[:, cs] * scale).astype(BF16)
        s = lax.dot_general(q_b, mk_ref[:, cs].astype(BF16), NT_DIMS, preferred_element_type=F32)
        m = jnp.max(s, axis=1, keepdims=True)
        p = jnp.exp(s - m)
        p = (p / jnp.sum(p, axis=1, keepdims=True)).astype(BF16)
        outs.append(jnp.dot(p, mv_ref[:, cs].astype(BF16), preferred_element_type=F32).astype(BF16))
    o = jnp.concatenate(outs, axis=1)
    o_ref[...] = x + jnp.dot(o, wob_ref[...], preferred_element_type=F32)


def _cross_prompt(x, gain, w_q, w_o, layer, mkv, B, S, *, CH, D, tm=256):
    dm = x.shape[1]
    n_mem = mkv.shape[0] // B
    tm = min(tm, S)
    ns = S // tm
    kern = functools.partial(_cross_kernel, CH=CH, D=D, scale=D ** -0.5)
    return pl.pallas_call(
        kern,
        grid=(B, ns),
        in_specs=[
            pl.BlockSpec((tm, dm), lambda b, i: (b * ns + i, 0)),
            pl.BlockSpec((1, dm), lambda b, i: (0, 0)),
            pl.BlockSpec((None, dm, CH * D), lambda b, i: (layer, 0, 0)),
            pl.BlockSpec((n_mem, CH * D), lambda b, i: (b, 0)),
            pl.BlockSpec((n_mem, CH * D), lambda b, i: (b, 1)),
            pl.BlockSpec((None, CH * D, dm), lambda b, i: (layer, 0, 0)),
        ],
        out_specs=pl.BlockSpec((tm, dm), lambda b, i: (b * ns + i, 0)),
        out_shape=jax.ShapeDtypeStruct((B * S, dm), F32),
        scratch_shapes=[pltpu.VMEM((dm, CH * D), BF16), pltpu.VMEM((CH * D, dm), BF16)],
        compiler_params=_params("arbitrary", "arbitrary"),
        name="cross_prompt",
    )(x, gain.reshape(1, dm), w_q, mkv, mkv, w_o)


def _fox_decode_kernel(pt_ref, q_ref, kx_ref, vx_ref, zg_ref, bf_ref, *rest, P, KV, G, nc, scale):
    k_refs, v_refs, lf_refs = rest[:P], rest[P:2 * P], rest[2 * P:3 * P]
    o_ref, lfo_ref, m_ref, l_ref, run_ref, acc_ref = rest[3 * P:]
    c = pl.program_id(1)
    R = KV * G
    q = q_ref[0] * scale

    @pl.when(c == 0)
    def _():
        lf_new = _log_sigmoid(zg_ref[0] + bf_ref[...])
        lfo_ref[0] = lf_new
        run_ref[...] = lf_new
        m_ref[...] = jnp.sum(q * kx_ref[0], axis=1, keepdims=True)
        l_ref[...] = jnp.ones(l_ref.shape, F32)
        acc_ref[...] = vx_ref[0]

    C = LANES * KV
    q_b = q.astype(BF16)
    own = (lax.broadcasted_iota(jnp.int32, (R, C), 1) % KV) == (lax.broadcasted_iota(jnp.int32, (R, C), 0) // G)
    lfs = [lf_refs[i][...] for i in range(P)]
    parts = [_split3(lf) for lf in lfs]
    stacked = jnp.concatenate([parts[i][t] for t in range(3) for i in range(P)], axis=0)
    r = lax.broadcasted_iota(jnp.int32, (LANES, C), 0)
    cc = lax.broadcasted_iota(jnp.int32, (LANES, C), 1)
    later = (r > cc // KV).astype(BF16)
    y = jnp.dot(stacked, later, preferred_element_type=F32)
    run = run_ref[...]
    s_list = []
    for i in range(P):
        w = y[i * R:(i + 1) * R] + y[(P + i) * R:(P + i + 1) * R] + y[(2 * P + i) * R:(2 * P + i + 1) * R]
        s = lax.dot_general(q_b, k_refs[i][...].astype(BF16), NT_DIMS, preferred_element_type=F32)
        s_list.append(jnp.where(own, s + (run + w), NEG))
        run = run + (w[:, 0:1] + lfs[i][:, 0:1])
    run_ref[...] = run
    s_all = jnp.concatenate(s_list, axis=1)
    m_prev = m_ref[...]
    m_new = jnp.maximum(m_prev, jnp.max(s_all, axis=1, keepdims=True))
    alpha = jnp.exp(m_prev - m_new)
    p = jnp.exp(s_all - m_new)
    l_ref[...] = alpha * l_ref[...] + jnp.sum(p, axis=1, keepdims=True)
    v_all = jnp.concatenate([v_refs[i][...] for i in range(P)], axis=0).astype(BF16)
    acc_ref[...] = alpha * acc_ref[...] + jnp.dot(p.astype(BF16), v_all, preferred_element_type=F32)
    m_ref[...] = m_new

    @pl.when(c == nc - 1)
    def _():
        o_ref[0] = acc_ref[...] / l_ref[...]


def _fox_decode(q, kx, vx, zg, b_f, k_pages, v_pages, lf_pages, page_table, layer, *, KV, G, D):
    NB, n_pages = page_table.shape
    H = KV * G
    P = min(FOX_PAGES_PER_STEP, n_pages)
    assert n_pages % P == 0 and k_pages.shape[2] == LANES * KV
    nc = n_pages // P

    def page_map(i):
        return lambda b, c, pt: (layer, pt[b, n_pages - 1 - (c * P + i)], 0, 0)

    head = lambda b, c, pt: (b, 0, 0)
    k_specs = [pl.BlockSpec((None, None, LANES * KV, D), page_map(i)) for i in range(P)]
    lf_specs = [pl.BlockSpec((None, None, H, LANES), page_map(i)) for i in range(P)]
    kern = functools.partial(_fox_decode_kernel, P=P, KV=KV, G=G, nc=nc, scale=D ** -0.5)
    return pl.pallas_call(
        kern,
        grid_spec=pltpu.PrefetchScalarGridSpec(
            num_scalar_prefetch=1,
            grid=(NB, nc),
            in_specs=[pl.BlockSpec((1, H, D), head)] * 3
            + [pl.BlockSpec((1, H, 1), head), pl.BlockSpec((H, 1), lambda b, c, pt: (0, 0))]
            + k_specs + k_specs + lf_specs,
            out_specs=[pl.BlockSpec((1, H, D), head), pl.BlockSpec((1, H, 1), head)],
            scratch_shapes=[pltpu.VMEM((H, 1), F32)] * 3 + [pltpu.VMEM((H, D), F32)],
        ),
        out_shape=[jax.ShapeDtypeStruct((NB, H, D), F32), jax.ShapeDtypeStruct((NB, H, 1), F32)],
        compiler_params=_params("arbitrary", "arbitrary"),
        name="fox_decode",
    )(page_table, q, kx, vx, zg, b_f.reshape(H, 1), *([k_pages] * P), *([v_pages] * P), *([lf_pages] * P))


def _diff_decode_kernel(pt_ref, q_ref, kx_ref, vx_ref, sl_ref, lam_ref, g_ref, *rest,
                        P, KV, G, nc, scale, past, lam_init):
    k_refs, v_refs = rest[:P], rest[P:2 * P]
    o_ref, m_ref, l_ref, acc_lo_ref, acc_hi_ref = rest[2 * P:]
    c = pl.program_id(1)
    H = KV * G
    R = 2 * H
    q = q_ref[0] * scale

    @pl.when(c == 0)
    def _():
        m_ref[...] = jnp.sum(q * kx_ref[0], axis=1, keepdims=True)
        l_ref[...] = jnp.ones(l_ref.shape, F32)
        acc_lo_ref[...] = vx_ref[0, :, 0:LANES]
        acc_hi_ref[...] = vx_ref[0, :, LANES:2 * LANES]

    T = 2 * KV
    C = LANES * T
    q_b = q.astype(BF16)
    row = lax.broadcasted_iota(jnp.int32, (R, C), 0)
    col = lax.broadcasted_iota(jnp.int32, (R, C), 1)
    own = (col % T) == (row // H) * KV + (row % H) // G
    pos_in_page = lax.broadcasted_iota(jnp.int32, (1, C), 1) // T
    s_list = []
    for i in range(P):
        s = lax.dot_general(q_b, k_refs[i][...].astype(BF16), NT_DIMS, preferred_element_type=F32)
        rel = (past - ((c * P + i) * LANES + pos_in_page)).astype(F32)
        s_list.append(jnp.where(own, s - sl_ref[...] * rel, NEG))
    s_all = jnp.concatenate(s_list, axis=1)
    m_prev = m_ref[...]
    m_new = jnp.maximum(m_prev, jnp.max(s_all, axis=1, keepdims=True))
    alpha = jnp.exp(m_prev - m_new)
    lane_row = lax.broadcasted_iota(jnp.int32, (R, LANES), 0)
    lane_half = (lax.broadcasted_iota(jnp.int32, (R, LANES), 1) % T) // KV
    p_sum = jnp.zeros((R, LANES), F32)
    lo_tiles, hi_tiles = [], []
    for t in range(P * T):
        p = jnp.exp(s_all[:, t * LANES:(t + 1) * LANES] - m_new)
        p_sum = p_sum + p
        p = p + jnp.where(lane_row < H, pltpu.roll(p, KV, 1), pltpu.roll(p, LANES - KV, 1))
        lo_tiles.append(jnp.where(lane_half == 0, p, 0.0))
        hi_tiles.append(jnp.where(lane_half == 1, p, 0.0))
    l_ref[...] = alpha * l_ref[...] + jnp.sum(p_sum, axis=1, keepdims=True)
    p_both = jnp.concatenate([jnp.concatenate(lo_tiles, axis=1), jnp.concatenate(hi_tiles, axis=1)],
                             axis=0).astype(BF16)
    v_all = jnp.concatenate([v_refs[i][...] for i in range(P)], axis=0).astype(BF16)
    o = jnp.dot(p_both, v_all, preferred_element_type=F32)
    acc_lo_ref[...] = alpha * acc_lo_ref[...] + o[0:R]
    acc_hi_ref[...] = alpha * acc_hi_ref[...] + o[R:2 * R]
    m_ref[...] = m_new

    @pl.when(c == nc - 1)
    def _():
        lam = _diff_lambda(lam_ref, lam_init)
        inv = 1.0 / l_ref[...]
        lo = acc_lo_ref[...] * inv
        hi = acc_hi_ref[...] * inv
        o = jnp.concatenate([lo[0:H] - lam * lo[H:R], hi[0:H] - lam * hi[H:R]], axis=1)
        o_ref[0] = _rms(o, g_ref[...]) * (1.0 - lam_init)


def _diff_decode(q, kx, vx, slopes_rows, lam_rows, subln, k_pages, v_pages, page_table, layer, *, KV, G, D,
                 lam_init):
    NB, n_pages = page_table.shape
    H = KV * G
    R = 2 * H
    P = min(DIFF_PAGES_PER_STEP, n_pages)
    assert n_pages % P == 0 and k_pages.shape[2] == LANES * 2 * KV
    nc = n_pages // P

    def page_map(i):
        return lambda b, c, pt: (layer, pt[b, c * P + i], 0, 0)

    head = lambda b, c, pt: (b, 0, 0)
    const = lambda b, c, pt: (0, 0)
    k_specs = [pl.BlockSpec((None, None, LANES * 2 * KV, D), page_map(i)) for i in range(P)]
    kern = functools.partial(_diff_decode_kernel, P=P, KV=KV, G=G, nc=nc, scale=D ** -0.5,
                             past=n_pages * LANES, lam_init=lam_init)
    return pl.pallas_call(
        kern,
        grid_spec=pltpu.PrefetchScalarGridSpec(
            num_scalar_prefetch=1,
            grid=(NB, nc),
            in_specs=[pl.BlockSpec((1, R, D), head), pl.BlockSpec((1, R, D), head), pl.BlockSpec((1, R, 2 * D), head),
                      pl.BlockSpec((R, 1), const), pl.BlockSpec((4, D), const), pl.BlockSpec((1, 2 * D), const)]
            + k_specs + k_specs,
            out_specs=pl.BlockSpec((1, H, 2 * D), head),
            scratch_shapes=[pltpu.VMEM((R, 1), F32)] * 2 + [pltpu.VMEM((R, D), F32)] * 2,
        ),
        out_shape=jax.ShapeDtypeStruct((NB, H, 2 * D), F32),
        compiler_params=_params("arbitrary", "arbitrary"),
        name="diff_decode",
    )(page_table, q, kx, vx, slopes_rows, lam_rows, subln.reshape(1, 2 * D), *([k_pages] * P), *([v_pages] * P))


def _swa_decode_kernel(q_ref, kt_ref, vt_ref, kx_ref, vx_ref, sl_ref, sk_ref, o_ref, *, H, KV, D, W, scale):
    G = H // KV
    q = q_ref[0] * scale
    s = jnp.dot(q.astype(BF16), kt_ref[0].astype(BF16), preferred_element_type=F32)
    rel = (W - lax.broadcasted_iota(jnp.int32, (1, W), 1)).astype(F32)
    s = s - sl_ref[...] * rel
    s_new = jnp.sum(q * kx_ref[0], axis=1, keepdims=True)
    snk = sk_ref[...]
    m = jnp.maximum(jnp.maximum(jnp.max(s, axis=1, keepdims=True), s_new), snk)
    p = jnp.exp(s - m)
    p_new = jnp.exp(s_new - m)
    den = jnp.sum(p, axis=1, keepdims=True) + p_new + jnp.exp(snk - m)
    o = lax.dot_general((p / den).astype(BF16), vt_ref[0].astype(BF16), NT_DIMS, preferred_element_type=F32)
    o = o + (p_new / den) * vx_ref[0]
    col_kv = lax.broadcasted_iota(jnp.int32, (H, KV * D), 1) // D
    row_kv = lax.broadcasted_iota(jnp.int32, (H, KV * D), 0) // G
    o = jnp.where(col_kv == row_kv, o, 0.0)
    folded = o[:, 0:LANES]
    for t in range(1, KV * D // LANES):
        folded = folded + o[:, t * LANES:(t + 1) * LANES]
    odd = (lax.broadcasted_iota(jnp.int32, (H, LANES), 0) // G) % 2 == 1
    o_ref[0] = jnp.where(odd, pltpu.roll(folded, LANES // 2, 1), folded)


def _swa_decode(q_exp, k_t, v_t, kx, vx, slopes, sinks, *, H, KV, D, W):
    NB = q_exp.shape[0]
    kern = functools.partial(_swa_decode_kernel, H=H, KV=KV, D=D, W=W, scale=D ** -0.5)
    row = lambda b: (b, 0, 0)
    return pl.pallas_call(
        kern,
        grid=(NB,),
        in_specs=[pl.BlockSpec((1, H, KV * D), row), pl.BlockSpec((1, KV * D, W), row),
                  pl.BlockSpec((1, KV * D, W), row), pl.BlockSpec((1, H, KV * D), row),
                  pl.BlockSpec((1, H, KV * D), row),
                  pl.BlockSpec((H, 1), lambda b: (0, 0)), pl.BlockSpec((H, 1), lambda b: (0, 0))],
        out_specs=pl.BlockSpec((1, H, LANES), row),
        out_shape=jax.ShapeDtypeStruct((NB, H, LANES), F32),
        compiler_params=_params("arbitrary"),
        name="swa_decode",
    )(q_exp, k_t, v_t, kx, vx, slopes.reshape(H, 1), sinks.reshape(H, 1))


def _cross_decode_kernel(q_ref, k_ref, v_ref, o_ref, *, CH, scale):
    q_b = (q_ref[0] * scale).astype(BF16)
    s = lax.dot_general(q_b, k_ref[0].astype(BF16), NT_DIMS, preferred_element_type=F32)
    row = lax.broadcasted_iota(jnp.int32, s.shape, 0)
    col = lax.broadcasted_iota(jnp.int32, s.shape, 1)
    s = jnp.where(col % CH == row, s, NEG)
    m = jnp.max(s, axis=1, keepdims=True)
    p = jnp.where(col % CH == row, jnp.exp(s - m), 0.0)
    den = jnp.maximum(jnp.sum(p, axis=1, keepdims=True), 1e-30)
    o = jnp.dot((p / den).astype(BF16), v_ref[0].astype(BF16), preferred_element_type=F32)
    o_ref[0] = o[0:CH]


def _cross_decode(q_pad, mem_k, mem_v, layer, *, CH, D):
    NB, R, _ = q_pad.shape
    n = mem_k.shape[2]
    kern = functools.partial(_cross_decode_kernel, CH=CH, scale=D ** -0.5)
    return pl.pallas_call(
        kern,
        grid=(NB,),
        in_specs=[pl.BlockSpec((1, R, D), lambda b: (b, 0, 0)),
                  pl.BlockSpec((None, 1, n, D), lambda b: (layer, b, 0, 0)),
                  pl.BlockSpec((None, 1, n, D), lambda b: (layer, b, 0, 0))],
        out_specs=pl.BlockSpec((1, CH, D), lambda b: (b, 0, 0)),
        out_shape=jax.ShapeDtypeStruct((NB, CH, D), F32),
        compiler_params=_params("arbitrary"),
        name="cross_decode",
    )(q_pad, mem_k, mem_v)


def _alibi_slopes(n):
    return jnp.exp2(-8.0 * jnp.arange(1, n + 1, dtype=F32) / n)


def kernel(x_prompt, x_sample, mem_prompt, cache_fox_k, cache_fox_v, cache_fox_logf, cache_diff_k, cache_diff_v, state_swa_k, state_swa_v, cache_mem_k, cache_mem_v, page_table, norm_mix, w_a_in, b_a_f, w_a_out, w_b_in, lam_q1, lam_k1, lam_q2, lam_k2, subln_b, w_b_out, w_c_in, b_c_in, sinks_c, w_c_out, norm_cross, norm_mem, w_x_q, w_x_kv, w_x_out, norm_ffn, w_ff_gu, w_ff_down, norm_final):
    B, S, dm = x_prompt.shape
    NB = x_sample.shape[0]
    depth = norm_mix.shape[0]
    n_mem = mem_prompt.shape[1]
    _, n_pool, page, KVA, HDA = cache_fox_k.shape
    HA = cache_fox_logf.shape[-1]
    KVB, HDB = cache_diff_k.shape[3], cache_diff_k.shape[4] // 2
    HB = dm // (2 * HDB)
    _, _, W, KVC, HDC = state_swa_k.shape
    HC = dm // HDC
    CH, CHD = cache_mem_k.shape[3], cache_mem_k.shape[4]
    assert page == LANES and x_sample.shape[1] == 1

    xp = x_prompt.reshape(B * S, dm)
    xs = x_sample.reshape(NB, dm)
    mem = mem_prompt.reshape(B * n_mem, dm)

    fox_k_pages = cache_fox_k.reshape(-1, n_pool, page * KVA, HDA)
    fox_v_pages = cache_fox_v.reshape(-1, n_pool, page * KVA, HDA)
    fox_lf_pages = jnp.swapaxes(cache_fox_logf, 2, 3)

    def diff_pages(c):
        c = c.reshape(-1, n_pool, page, KVB, 2, HDB)
        return jnp.transpose(c, (0, 1, 2, 4, 3, 5)).reshape(-1, n_pool, page * 2 * KVB, HDB)

    diff_k_pages, diff_v_pages = diff_pages(cache_diff_k), diff_pages(cache_diff_v)
    mem_k_rows = cache_mem_k.reshape(depth, NB, n_mem * CH, CHD)
    mem_v_rows = cache_mem_v.reshape(depth, NB, n_mem * CH, CHD)

    slopes_b = _alibi_slopes(HB)
    slopes_c = _alibi_slopes(HC)

    pfk, pfv, pfl, sfk, sfv, sfl = [], [], [], [], [], []
    pdk, pdv, sdk, sdv = [], [], [], []
    pck, pcv, sck, scv = [], [], [], []
    pmk, pmv = [], []
    ia = ib = ic = 0
    for i in range(depth):
        kind = i % 3
        if kind == 0:
            nq, nkv = HA * HDA, KVA * HDA
            G = HA // KVA
            z = _mm(xp, w_a_in, ia, gain=norm_mix[i])
            zs = _mm(xs, w_a_in, ia, gain=norm_mix[i])
            zg_t = jnp.swapaxes(z[:, nq + 2 * nkv:].reshape(B, S, HA), 1, 2)
            lf_t, c_t = _fox_gate(zg_t, b_a_f[ia])
            o = _fox_flash(z, c_t.reshape(B, KVA, G, S), B, S, H=HA, KV=KVA, D=HDA)
            xp = _mm(o, w_a_out, ia, residual=xp)
            pfk.append(z[:, nq:nq + nkv].reshape(B, S, KVA, HDA))
            pfv.append(z[:, nq + nkv:nq + 2 * nkv].reshape(B, S, KVA, HDA))
            pfl.append(jnp.swapaxes(lf_t, 1, 2))

            ks = zs[:, nq:nq + nkv].reshape(NB, KVA, HDA)
            vs = zs[:, nq + nkv:nq + 2 * nkv].reshape(NB, KVA, HDA)
            os_, lfs = _fox_decode(
                zs[:, :nq].reshape(NB, HA, HDA), jnp.repeat(ks, G, axis=1), jnp.repeat(vs, G, axis=1),
                zs[:, nq + 2 * nkv:].reshape(NB, HA, 1), b_a_f[ia],
                fox_k_pages, fox_v_pages, fox_lf_pages, page_table, ia, KV=KVA, G=G, D=HDA)
            xs = _mm(os_.reshape(NB, nq), w_a_out, ia, residual=xs)
            sfk.append(ks.reshape(NB, 1, KVA, HDA))
            sfv.append(vs.reshape(NB, 1, KVA, HDA))
            sfl.append(lfs.reshape(NB, 1, HA))
            ia += 1
        elif kind == 1:
            lam_init = 0.8 - 0.6 * math.exp(-0.3 * i)
            nq, nkv = HB * 2 * HDB, KVB * 2 * HDB
            G = HB // KVB
            lam_rows = jnp.stack([lam_q1[ib], lam_k1[ib], lam_q2[ib], lam_k2[ib]])
            z = _mm(xp, w_b_in, ib, gain=norm_mix[i])
            zs = _mm(xs, w_b_in, ib, gain=norm_mix[i])
            o = _diff_flash(z, slopes_b, lam_rows, subln_b[ib], B, S, H=HB, KV=KVB, D=HDB, lam_init=lam_init)
            xp = _mm(o, w_b_out, ib, residual=xp)
            pdk.append(z[:, nq:nq + nkv].reshape(B, S, KVB, 2 * HDB))
            pdv.append(z[:, nq + nkv:].reshape(B, S, KVB, 2 * HDB))

            ks = zs[:, nq:nq + nkv].reshape(NB, KVB, 2, HDB)
            vs = zs[:, nq + nkv:].reshape(NB, KVB, 2 * HDB)
            q_rows = jnp.transpose(zs[:, :nq].reshape(NB, HB, 2, HDB), (0, 2, 1, 3)).reshape(NB, 2 * HB, HDB)
            kx = jnp.transpose(jnp.repeat(ks, G, axis=1), (0, 2, 1, 3)).reshape(NB, 2 * HB, HDB)
            vx = jnp.tile(jnp.repeat(vs, G, axis=1), (1, 2, 1))
            os_ = _diff_decode(q_rows, kx, vx, jnp.tile(slopes_b, 2).reshape(2 * HB, 1), lam_rows, subln_b[ib],
                               diff_k_pages, diff_v_pages, page_table, ib, KV=KVB, G=G, D=HDB, lam_init=lam_init)
            xs = _mm(os_.reshape(NB, nq), w_b_out, ib, residual=xs)
            sdk.append(ks.reshape(NB, 1, KVB, 2 * HDB))
            sdv.append(vs.reshape(NB, 1, KVB, 2 * HDB))
            ib += 1
        else:
            nq, nkv = HC * HDC, KVC * HDC
            G = HC // KVC
            z = _mm(xp, w_c_in, ic, gain=norm_mix[i], bias=b_c_in[ic])
            zs = _mm(xs, w_c_in, ic, gain=norm_mix[i], bias=b_c_in[ic])
            o = _swa_flash(z, slopes_c, sinks_c[ic], B, S, H=HC, KV=KVC, D=HDC, W=W)
            xp = _mm(o, w_c_out, ic, residual=xp)
            z3 = z.reshape(B, S, nq + 2 * nkv)
            pck.append(z3[:, S - W:, nq:nq + nkv].reshape(B, W, KVC, HDC))
            pcv.append(z3[:, S - W:, nq + nkv:].reshape(B, W, KVC, HDC))

            ks = zs[:, nq:nq + nkv].reshape(NB, 1, KVC, HDC)
            vs = zs[:, nq + nkv:].reshape(NB, 1, KVC, HDC)
            blk = (jnp.arange(HC)[:, None] // G == jnp.arange(KVC)[None, :]).astype(F32)
            q_exp = (zs[:, :nq].reshape(NB, HC, 1, HDC) * blk[None, :, :, None]).reshape(NB, HC, nkv)
            kx = jnp.broadcast_to(zs[:, None, nq:nq + nkv], (NB, HC, nkv))
            vx = jnp.broadcast_to(zs[:, None, nq + nkv:], (NB, HC, nkv))
            k_t = jnp.transpose(state_swa_k[ic], (0, 2, 3, 1)).reshape(NB, nkv, W)
            v_t = jnp.transpose(state_swa_v[ic], (0, 2, 3, 1)).reshape(NB, nkv, W)
            o2 = _swa_decode(q_exp, k_t, v_t, kx, vx, slopes_c, sinks_c[ic], H=HC, KV=KVC, D=HDC, W=W)
            xs = _mm(o2[:, :, :HDC].reshape(NB, nq), w_c_out, ic, residual=xs)
            sck.append(jnp.concatenate([state_swa_k[ic][:, 1:], ks], axis=1))
            scv.append(jnp.concatenate([state_swa_v[ic][:, 1:], vs], axis=1))
            ic += 1

        mkv = _mm(mem, w_x_kv, i, gain=norm_mem[i])
        xp = _cross_prompt(xp, norm_cross[i], w_x_q, w_x_out, i, mkv, B, S, CH=CH, D=CHD)
        pmk.append(mkv[:, :CH * CHD].reshape(B, n_mem, CH, CHD))
        pmv.append(mkv[:, CH * CHD:].reshape(B, n_mem, CH, CHD))
        qs = _mm(xs, w_x_q, i, gain=norm_cross[i]).reshape(NB, CH, CHD)
        q_pad = jnp.pad(qs, ((0, 0), (0, 16 - CH), (0, 0)))
        oc = _cross_decode(q_pad, mem_k_rows, mem_v_rows, i, CH=CH, D=CHD)
        xs = _mm(oc.reshape(NB, CH * CHD), w_x_out, i, residual=xs)

        h = _mm(xp, w_ff_gu, i, gain=norm_ffn[i], glu=True, out_dtype=BF16)
        xp = _mm(h, w_ff_down, i, residual=xp, tn=256)
        hs = _mm(xs, w_ff_gu, i, gain=norm_ffn[i], glu=True, out_dtype=BF16)
        xs = _mm(hs, w_ff_down, i, residual=xs, tn=256)

    y_prompt = _rmsnorm(xp, norm_final).reshape(B, S, dm)
    y_sample = _rmsnorm(xs, norm_final).reshape(NB, 1, dm)
    return (y_prompt, y_sample,
            jnp.stack(pfk), jnp.stack(pfv), jnp.stack(pfl),
            jnp.stack(pdk), jnp.stack(pdv),
            jnp.stack(pck), jnp.stack(pcv),
            jnp.stack(pmk), jnp.stack(pmv),
            jnp.stack(sfk), jnp.stack(sfv), jnp.stack(sfl),
            jnp.stack(sdk), jnp.stack(sdv),
            jnp.stack(sck), jnp.stack(scv))
```

```python
import functools
import math

import jax
import jax.numpy as jnp
from jax import lax
from jax.experimental import pallas as pl
from jax.experimental.pallas import tpu as pltpu

F32 = jnp.float32
BF16 = jnp.bfloat16
EPS = 1e-6
NEG = -1e30
LOG2E = math.log2(math.e)
LANES = 128
VMEM_LIMIT_BYTES = 56 * 1024 * 1024
FOX_PAGES_PER_STEP = 32
DIFF_PAGES_PER_STEP = 8
NT_DIMS = (((1,), (1,)), ((), ()))


def _params(*sem):
    return pltpu.CompilerParams(dimension_semantics=sem, vmem_limit_bytes=VMEM_LIMIT_BYTES)


def _rms(x, g):
    return x * lax.rsqrt(jnp.mean(x * x, axis=-1, keepdims=True) + EPS) * g


def _log_sigmoid(x):
    return jnp.minimum(x, 0.0) - jnp.log(1.0 + jnp.exp(-jnp.abs(x)))


def _split3(x):
    hi = x.astype(BF16)
    r1 = x - hi.astype(F32)
    mid = r1.astype(BF16)
    lo = (r1 - mid.astype(F32)).astype(BF16)
    return hi, mid, lo


def _mm_kernel(*refs, has_gain, has_bias, has_res, glu, stage_x, tm, rc, line_plans, n_alias):
    refs = list(refs)
    x_ref = refs.pop(0)
    g_ref = refs.pop(0) if has_gain else None
    w_ref = refs.pop(0)
    w2_ref = refs.pop(0) if glu else None
    b_ref = refs.pop(0) if has_bias else None
    r_ref = refs.pop(0) if has_res else None
    del refs[:n_alias]
    o_ref = refs.pop(0)
    line_refs = [refs.pop(0) for _ in line_plans]
    xs_ref = refs.pop(0) if stage_x else None

    if stage_x:
        @pl.when(pl.program_id(1) == 0)
        def _():
            def body(c, carry):
                r0 = pl.multiple_of(c * rc, rc)
                xc = x_ref[pl.ds(r0, rc), :].astype(F32)
                if has_gain:
                    xc = _rms(xc, g_ref[...])
                xs_ref[pl.ds(r0, rc), :] = xc.astype(BF16)
                return carry

            lax.fori_loop(0, tm // rc, body, 0)

        xb = xs_ref[...]
    else:
        xb = x_ref[...]
    acc = jnp.dot(xb, w_ref[...].astype(BF16), preferred_element_type=F32)
    if glu:
        up = jnp.dot(xb, w2_ref[...].astype(BF16), preferred_element_type=F32)
        acc = acc * jax.nn.sigmoid(acc) * up
    if has_bias:
        acc = acc + b_ref[...]
    if has_res:
        acc = r_ref[...] + acc
    o_ref[...] = acc.astype(o_ref.dtype)
    for ref, (T, per_tile) in zip(line_refs, line_plans):
        for jj, chunks in per_tile.items():
            @pl.when(pl.program_id(1) == jj)
            def _(ref=ref, T=T, chunks=chunks):
                for t, offset in chunks:
                    ref[pl.ds(offset, tm, stride=T), :] = o_ref[:, t * LANES:(t + 1) * LANES]


def _mm(x, w, layer, *, gain=None, bias=None, residual=None, glu=False, out_dtype=F32, tm=1024, tn=512, lines=()):
    M, K = x.shape
    N = w.shape[-1] // (2 if glu else 1)
    tm = min(tm, M)
    assert M % tm == 0 and w.shape[-2] == K
    nj = pl.cdiv(N, tn)
    assert not glu or N % tn == 0
    stage_x = x.dtype != BF16 or gain is not None
    rc = min(tm, 128)
    in_specs = [pl.BlockSpec((tm, K), lambda i, j: (i, 0))]
    args = [x]
    if gain is not None:
        in_specs.append(pl.BlockSpec((1, K), lambda i, j: (0, 0)))
        args.append(gain.reshape(1, K).astype(F32))
    in_specs.append(pl.BlockSpec((None, K, tn), lambda i, j: (layer, 0, j)))
    args.append(w)
    if glu:
        in_specs.append(pl.BlockSpec((None, K, tn), lambda i, j: (layer, 0, j + nj)))
        args.append(w)
    if bias is not None:
        in_specs.append(pl.BlockSpec((1, tn), lambda i, j: (0, j)))
        args.append(bias.reshape(1, N).astype(F32))
    if residual is not None:
        in_specs.append(pl.BlockSpec((tm, tn), lambda i, j: (i, j)))
        args.append(residual)
    out_specs = [pl.BlockSpec((tm, tn), lambda i, j: (i, j))]
    out_shape = [jax.ShapeDtypeStruct((M, N), out_dtype)]
    aliases, line_plans = {}, []
    n_alias = 0
    for buf, nslots, slot, col0, offsets, T in lines:
        assert out_dtype == F32 and col0 % LANES == 0
        per_tile = {}
        for c, offset in enumerate(offsets):
            col = col0 + c * LANES
            per_tile.setdefault(col // tn, []).append(((col % tn) // LANES, offset))
        line_plans.append((T, per_tile))
        if buf is not None:
            assert buf.shape == (nslots, M * T, LANES)
            aliases[len(args)] = len(out_shape)
            in_specs.append(pl.BlockSpec(memory_space=pl.ANY))
            args.append(buf)
            n_alias += 1
        out_specs.append(pl.BlockSpec((None, tm * T, LANES), lambda i, j, slot=slot: (slot, i, 0)))
        out_shape.append(jax.ShapeDtypeStruct((nslots, M * T, LANES), F32))
    kern = functools.partial(_mm_kernel, has_gain=gain is not None, has_bias=bias is not None,
                             has_res=residual is not None, glu=glu, stage_x=stage_x, tm=tm, rc=rc,
                             line_plans=line_plans, n_alias=n_alias)
    outs = pl.pallas_call(
        kern,
        grid=(M // tm, nj),
        in_specs=in_specs,
        out_specs=out_specs,
        out_shape=out_shape,
        input_output_aliases=aliases,
        scratch_shapes=[pltpu.VMEM((tm, K), BF16)] if stage_x else [],
        compiler_params=_params("arbitrary", "arbitrary"),
        name=f"mm{'_glu' if glu else ''}_{M}x{K}x{N}",
    )(*args)
    return outs if lines else outs[0]


def _rmsnorm_kernel(x_ref, g_ref, o_ref):
    o_ref[...] = _rms(x_ref[...], g_ref[...])


def _rmsnorm(x, g, tm=256):
    M, K = x.shape
    tm = min(tm, M)
    return pl.pallas_call(
        _rmsnorm_kernel,
        grid=(M // tm,),
        in_specs=[pl.BlockSpec((tm, K), lambda i: (i, 0)), pl.BlockSpec((1, K), lambda i: (0, 0))],
        out_specs=pl.BlockSpec((tm, K), lambda i: (i, 0)),
        out_shape=jax.ShapeDtypeStruct((M, K), F32),
        compiler_params=_params("arbitrary"),
        name="final_norm",
    )(x, g.reshape(1, K))


def _lanes(x, width):
    return x if width == LANES else pltpu.repeat(x, width // LANES, axis=1)


def _online_update(s, v_b, m_ref, l_ref, acc_ref, idx, col0, dv):
    m_prev = m_ref[idx]
    m_new = jnp.maximum(m_prev, jnp.max(s, axis=1, keepdims=True))
    alpha = jnp.exp2(m_prev - m_new)
    p = jnp.exp2(s - _lanes(m_new, s.shape[1]))
    l_ref[idx] = alpha * l_ref[idx] + jnp.sum(p, axis=1, keepdims=True)
    acc_ref[:, col0:col0 + dv] = _lanes(alpha, dv) * acc_ref[:, col0:col0 + dv] + jnp.dot(
        p.astype(BF16), v_b, preferred_element_type=F32)
    m_ref[idx] = m_new


def _diag_mask(t):
    row = lax.broadcasted_iota(jnp.int32, (t, t), 0)
    col = lax.broadcasted_iota(jnp.int32, (t, t), 1)
    return col <= row


def _fox_gate_kernel(zg_ref, b_ref, lf_ref, c_ref, *, S):
    r = lax.broadcasted_iota(jnp.int32, (LANES, LANES), 0)
    c_ = lax.broadcasted_iota(jnp.int32, (LANES, LANES), 1)
    tri = (r <= c_).astype(BF16)
    H = zg_ref.shape[0]
    carry = jnp.zeros((H, 1), F32)
    for t in range(S // LANES):
        sl = slice(t * LANES, (t + 1) * LANES)
        lf = _log_sigmoid(zg_ref[:, sl] + b_ref[...])
        lf_ref[:, sl] = lf
        hi, mid, lo = _split3(lf)
        y = jnp.dot(jnp.concatenate([hi, mid, lo], axis=0), tri, preferred_element_type=F32)
        c = y[0:H] + y[H:2 * H] + y[2 * H:3 * H] + carry
        c_ref[:, sl] = c
        carry = c[:, LANES - 1:LANES]


def _fox_gate(zg_t, b_f):
    B, H, S = zg_t.shape
    spec = pl.BlockSpec((None, H, S), lambda b: (b, 0, 0))
    return pl.pallas_call(
        functools.partial(_fox_gate_kernel, S=S),
        grid=(B,),
        in_specs=[spec, pl.BlockSpec((H, 1), lambda b: (0, 0))],
        out_specs=[spec, spec],
        out_shape=[jax.ShapeDtypeStruct((B, H, S), F32)] * 2,
        compiler_params=_params("arbitrary"),
        name="fox_gate",
    )(zg_t, b_f.reshape(H, 1))


def _flash_init(q_ref, qb_ref, m_ref, l_ref, acc_ref, scale):
    qb_ref[...] = (q_ref[...] * (scale * LOG2E)).astype(BF16)
    m_ref[...] = jnp.full(m_ref.shape, NEG, F32)
    l_ref[...] = jnp.zeros(l_ref.shape, F32)
    acc_ref[...] = jnp.zeros(acc_ref.shape, F32)


def _causal_pairs(nq):
    pairs = [(qi, ki) for qi in range(nq) for ki in range(qi + 1)]
    return (jnp.asarray([p[0] for p in pairs], jnp.int32), jnp.asarray([p[1] for p in pairs], jnp.int32))


def _fox_flash_kernel(qt_ref, kt_ref, q_ref, k_ref, v_ref, ck_ref, o_ref, qb_ref, m_ref, l_ref, acc_ref, *,
                      G, D, t, scale):
    qi, ki = qt_ref[pl.program_id(2)], kt_ref[pl.program_id(2)]

    @pl.when(ki == 0)
    def _():
        _flash_init(q_ref, qb_ref, m_ref, l_ref, acc_ref, scale)

    def step(diagonal):
        k_b = k_ref[...].astype(BF16)
        v_b = v_ref[...].astype(BF16)
        for g in range(G):
            s = lax.dot_general(qb_ref[:, g * D:(g + 1) * D], k_b, NT_DIMS, preferred_element_type=F32)
            s = s - ck_ref[g:g + 1, :] * LOG2E
            if diagonal:
                s = jnp.where(_diag_mask(t), s, NEG)
            _online_update(s, v_b, m_ref, l_ref, acc_ref, g, g * D, D)

    @pl.when(ki < qi)
    def _():
        step(False)

    @pl.when(ki == qi)
    def _():
        step(True)
        for g in range(G):
            o_ref[:, g * D:(g + 1) * D] = (acc_ref[:, g * D:(g + 1) * D] / l_ref[g]).astype(o_ref.dtype)


def _fox_flash(z, c_t, B, S, *, H, KV, D, t=512):
    G = H // KV
    t = min(t, S)
    nq = S // t
    qt, kt = _causal_pairs(nq)
    kern = functools.partial(_fox_flash_kernel, G=G, D=D, t=t, scale=D ** -0.5)
    return pl.pallas_call(
        kern,
        grid_spec=pltpu.PrefetchScalarGridSpec(
            num_scalar_prefetch=2,
            grid=(B, KV, qt.shape[0]),
            in_specs=[
                pl.BlockSpec((t, G * D), lambda b, j, n, qt, kt: (b * nq + qt[n], j)),
                pl.BlockSpec((t, D), lambda b, j, n, qt, kt: (b * nq + kt[n], H + j)),
                pl.BlockSpec((t, D), lambda b, j, n, qt, kt: (b * nq + kt[n], H + KV + j)),
                pl.BlockSpec((None, None, G, t), lambda b, j, n, qt, kt: (b, j, 0, kt[n])),
            ],
            out_specs=pl.BlockSpec((t, G * D), lambda b, j, n, qt, kt: (b * nq + qt[n], j)),
            scratch_shapes=[pltpu.VMEM((t, G * D), BF16), pltpu.VMEM((G, t, LANES), F32),
                            pltpu.VMEM((G, t, LANES), F32), pltpu.VMEM((t, G * D), F32)],
        ),
        out_shape=jax.ShapeDtypeStruct((B * S, H * D), BF16),
        compiler_params=_params("arbitrary", "arbitrary", "arbitrary"),
        name="fox_flash",
    )(qt, kt, z, z, z, c_t)


def _diff_lambda(lam_ref, lam_init):
    a = jnp.sum(lam_ref[0:1, :] * lam_ref[1:2, :], axis=1, keepdims=True)
    b = jnp.sum(lam_ref[2:3, :] * lam_ref[3:4, :], axis=1, keepdims=True)
    return jnp.exp(a) - jnp.exp(b) + lam_init


def _diff_flash_kernel(qt_ref, kt_ref, sl_ref, q_ref, k_ref, v_ref, lam_ref, g_ref, o_ref, qb_ref, m_ref, l_ref,
                       acc_ref, *, G, D, t, scale, lam_init):
    j = pl.program_id(1)
    qi, ki = qt_ref[pl.program_id(2)], kt_ref[pl.program_id(2)]
    DV = 2 * D

    @pl.when(ki == 0)
    def _():
        _flash_init(q_ref, qb_ref, m_ref, l_ref, acc_ref, scale)

    def step(diagonal):
        v_b = v_ref[...].astype(BF16)
        kpos = (lax.broadcasted_iota(jnp.int32, (1, t), 1) + (ki - qi) * t).astype(F32) * LOG2E
        for mp in range(2):
            k_b = k_ref[:, mp * D:(mp + 1) * D].astype(BF16)
            for g in range(G):
                slot = g * 2 + mp
                s = lax.dot_general(qb_ref[:, slot * D:(slot + 1) * D], k_b, NT_DIMS, preferred_element_type=F32)
                s = s + sl_ref[j * G + g] * kpos
                if diagonal:
                    s = jnp.where(_diag_mask(t), s, NEG)
                _online_update(s, v_b, m_ref, l_ref, acc_ref, slot, slot * DV, DV)

    @pl.when(ki < qi)
    def _():
        step(False)

    @pl.when(ki == qi)
    def _():
        step(True)
        lam = _diff_lambda(lam_ref, lam_init)
        for g in range(G):
            s1, s2 = g * 2, g * 2 + 1
            o = (acc_ref[:, s1 * DV:(s1 + 1) * DV] / _lanes(l_ref[s1], DV)
                 - lam * (acc_ref[:, s2 * DV:(s2 + 1) * DV] / _lanes(l_ref[s2], DV)))
            o = _rms(o, g_ref[...]) * (1.0 - lam_init)
            o_ref[:, g * DV:(g + 1) * DV] = o.astype(o_ref.dtype)


def _diff_flash(z, slopes, lam_rows, subln, B, S, *, H, KV, D, lam_init, t=512):
    G = H // KV
    DV = 2 * D
    t = min(t, S)
    nq = S // t
    qt, kt = _causal_pairs(nq)
    kern = functools.partial(_diff_flash_kernel, G=G, D=D, t=t, scale=D ** -0.5, lam_init=lam_init)
    return pl.pallas_call(
        kern,
        grid_spec=pltpu.PrefetchScalarGridSpec(
            num_scalar_prefetch=2,
            grid=(B, KV, qt.shape[0]),
            in_specs=[
                pl.BlockSpec(memory_space=pltpu.SMEM),
                pl.BlockSpec((t, G * DV), lambda b, j, n, qt, kt: (b * nq + qt[n], j)),
                pl.BlockSpec((t, DV), lambda b, j, n, qt, kt: (b * nq + kt[n], H + j)),
                pl.BlockSpec((t, DV), lambda b, j, n, qt, kt: (b * nq + kt[n], H + KV + j)),
                pl.BlockSpec((4, D), lambda b, j, n, qt, kt: (0, 0)),
                pl.BlockSpec((1, DV), lambda b, j, n, qt, kt: (0, 0)),
            ],
            out_specs=pl.BlockSpec((t, G * DV), lambda b, j, n, qt, kt: (b * nq + qt[n], j)),
            scratch_shapes=[pltpu.VMEM((t, G * DV), BF16), pltpu.VMEM((2 * G, t, LANES), F32),
                            pltpu.VMEM((2 * G, t, LANES), F32), pltpu.VMEM((t, 2 * G * DV), F32)],
        ),
        out_shape=jax.ShapeDtypeStruct((B * S, H * DV), BF16),
        compiler_params=_params("arbitrary", "arbitrary", "arbitrary"),
        name="diff_flash",
    )(qt, kt, slopes, z, z, z, lam_rows, subln.reshape(1, DV))


def _swa_flash_kernel(sl_ref, sk_ref, q_ref, kc_ref, kp_ref, vc_ref, vp_ref, o_ref, *, H, KV, W, scale):
    n = pl.program_id(1)
    G = H // KV
    half = LANES // 2
    in_lo = lax.broadcasted_iota(jnp.int32, (1, LANES), 1) < half
    i = lax.broadcasted_iota(jnp.int32, (W, 2 * W), 0)
    jj = lax.broadcasted_iota(jnp.int32, (W, 2 * W), 1)
    rel = W + i - jj
    mask = (rel >= 0) & (rel <= W) & ((jj >= W) | (n > 0))
    key_term = lax.broadcasted_iota(jnp.int32, (1, 2 * W), 1).astype(F32) * LOG2E
    row_term = (lax.broadcasted_iota(jnp.int32, (W, 1), 0) + W).astype(F32) * LOG2E

    for pair in range(KV // 2):
        cs = slice(pair * LANES, (pair + 1) * LANES)
        k_b = jnp.concatenate([kp_ref[:, cs], kc_ref[:, cs]], axis=0).astype(BF16)
        v_b = jnp.concatenate([vp_ref[:, cs], vc_ref[:, cs]], axis=0).astype(BF16)
        for e in range(2):
            j = 2 * pair + e
            on_kv = in_lo if e == 0 else ~in_lo
            for hp in range(G // 2):
                cb = j * (G // 2) + hp
                qp = q_ref[:, cb * LANES:(cb + 1) * LANES] * (scale * LOG2E)
                qp_swapped = pltpu.roll(qp, half, 1)
                outs = []
                for hh in range(2):
                    h = 2 * cb + hh
                    q_b = jnp.where(on_kv, qp if hh == e else qp_swapped, 0.0).astype(BF16)
                    s = lax.dot_general(q_b, k_b, NT_DIMS, preferred_element_type=F32)
                    s = jnp.where(mask, s + sl_ref[h] * key_term, NEG)
                    snk = sk_ref[h] * LOG2E + sl_ref[h] * row_term
                    m = jnp.maximum(jnp.max(s, axis=1, keepdims=True), snk)
                    p = jnp.exp2(s - m)
                    den = jnp.sum(p, axis=1, keepdims=True) + jnp.exp2(snk - m)
                    r = jnp.dot(p.astype(BF16), v_b, preferred_element_type=F32) * (1.0 / den)
                    outs.append(r if hh == e else pltpu.roll(r, half, 1))
                o_ref[:, cb * LANES:(cb + 1) * LANES] = jnp.where(in_lo, outs[0], outs[1]).astype(o_ref.dtype)


def _swa_flash(z, slopes, sinks, B, S, *, H, KV, D, W):
    assert D * 2 == LANES and W == LANES and KV % 2 == 0 and (H // KV) % 2 == 0
    nb = S // W
    qw, kw = H * D, KV * D
    kblk = qw // kw
    kern = functools.partial(_swa_flash_kernel, H=H, KV=KV, W=W, scale=D ** -0.5)
    smem = pl.BlockSpec(memory_space=pltpu.SMEM)
    return pl.pallas_call(
        kern,
        grid=(B, nb),
        in_specs=[
            smem, smem,
            pl.BlockSpec((W, qw), lambda b, n: (b * nb + n, 0)),
            pl.BlockSpec((W, kw), lambda b, n: (b * nb + n, kblk)),
            pl.BlockSpec((W, kw), lambda b, n: (b * nb + jnp.maximum(n - 1, 0), kblk)),
            pl.BlockSpec((W, kw), lambda b, n: (b * nb + n, kblk + 1)),
            pl.BlockSpec((W, kw), lambda b, n: (b * nb + jnp.maximum(n - 1, 0), kblk + 1)),
        ],
        out_specs=pl.BlockSpec((W, qw), lambda b, n: (b * nb + n, 0)),
        out_shape=jax.ShapeDtypeStruct((B * S, qw), BF16),
        compiler_params=_params("arbitrary", "arbitrary"),
        name="swa_flash",
    )(slopes, sinks, z, z, z, z, z)


def _cross_kernel(x_ref, g_ref, wq_ref, mk_ref, mv_ref, wo_ref, o_ref, wqb_ref, wob_ref, *, CH, D, scale):
    @pl.when((pl.program_id(0) == 0) & (pl.program_id(1) == 0))
    def _():
        wqb_ref[...] = wq_ref[...].astype(BF16)
        wob_ref[...] = wo_ref[...].astype(BF16)

    x = x_ref[...]
    xn = _rms(x, g_ref[...]).astype(BF16)
    q = jnp.dot(xn, wqb_ref[...], preferred_element_type=F32)
    outs = []
    for n in range(CH):
        cs = slice(n * D, (n + 1) * D)
        q_b = (q[:, cs] * scale).astype(BF16)
        s = lax.dot_general(q_b, mk_ref[:, cs].astype(BF16), NT_DIMS, preferred_element_type=F32)
        m = jnp.max(s, axis=1, keepdims=True)
        p = jnp.exp(s - m)
        p = (p / jnp.sum(p, axis=1, keepdims=True)).astype(BF16)
        outs.append(jnp.dot(p, mv_ref[:, cs].astype(BF16), preferred_element_type=F32).astype(BF16))
    o = jnp.concatenate(outs, axis=1)
    o_ref[...] = x + jnp.dot(o, wob_ref[...], preferred_element_type=F32)


def _cross_prompt(x, gain, w_q, w_o, layer, mkv, B, S, *, CH, D, tm=256):
    dm = x.shape[1]
    n_mem = mkv.shape[0] // B
    tm = min(tm, S)
    ns = S // tm
    kern = functools.partial(_cross_kernel, CH=CH, D=D, scale=D ** -0.5)
    return pl.pallas_call(
        kern,
        grid=(B, ns),
        in_specs=[
            pl.BlockSpec((tm, dm), lambda b, i: (b * ns + i, 0)),
            pl.BlockSpec((1, dm), lambda b, i: (0, 0)),
            pl.BlockSpec((None, dm, CH * D), lambda b, i: (layer, 0, 0)),
            pl.BlockSpec((n_mem, CH * D), lambda b, i: (b, 0)),
            pl.BlockSpec((n_mem, CH * D), lambda b, i: (b, 1)),
            pl.BlockSpec((None, CH * D, dm), lambda b, i: (layer, 0, 0)),
        ],
        out_specs=pl.BlockSpec((tm, dm), lambda b, i: (b * ns + i, 0)),
        out_shape=jax.ShapeDtypeStruct((B * S, dm), F32),
        scratch_shapes=[pltpu.VMEM((dm, CH * D), BF16), pltpu.VMEM((CH * D, dm), BF16)],
        compiler_params=_params("arbitrary", "arbitrary"),
        name="cross_prompt",
    )(x, gain.reshape(1, dm), w_q, mkv, mkv, w_o)


def _fox_decode_kernel(pt_ref, q_ref, kx_ref, vx_ref, zg_ref, bf_ref, *rest, P, KV, G, nc, scale):
    k_refs, v_refs, lf_refs = rest[:P], rest[P:2 * P], rest[2 * P:3 * P]
    o_ref, lfo_ref, m_ref, l_ref, run_ref, acc_ref = rest[3 * P:]
    c = pl.program_id(1)
    R = KV * G
    q = q_ref[0] * scale

    @pl.when(c == 0)
    def _():
        lf_new = _log_sigmoid(zg_ref[0] + bf_ref[...])
        lfo_ref[0] = lf_new
        run_ref[...] = lf_new
        m_ref[...] = jnp.sum(q * kx_ref[0], axis=1, keepdims=True)
        l_ref[...] = jnp.ones(l_ref.shape, F32)
        acc_ref[...] = vx_ref[0]

    C = LANES * KV
    q_b = q.astype(BF16)
    own = (lax.broadcasted_iota(jnp.int32, (R, C), 1) % KV) == (lax.broadcasted_iota(jnp.int32, (R, C), 0) // G)
    lfs = [lf_refs[i][...] for i in range(P)]
    parts = [_split3(lf) for lf in lfs]
    stacked = jnp.concatenate([parts[i][t] for t in range(3) for i in range(P)], axis=0)
    r = lax.broadcasted_iota(jnp.int32, (LANES, C), 0)
    cc = lax.broadcasted_iota(jnp.int32, (LANES, C), 1)
    later = (r > cc // KV).astype(BF16)
    y = jnp.dot(stacked, later, preferred_element_type=F32)
    run = run_ref[...]
    s_list = []
    for i in range(P):
        w = y[i * R:(i + 1) * R] + y[(P + i) * R:(P + i + 1) * R] + y[(2 * P + i) * R:(2 * P + i + 1) * R]
        s = lax.dot_general(q_b, k_refs[i][...].astype(BF16), NT_DIMS, preferred_element_type=F32)
        s_list.append(jnp.where(own, s + (run + w), NEG))
        run = run + (w[:, 0:1] + lfs[i][:, 0:1])
    run_ref[...] = run
    s_all = jnp.concatenate(s_list, axis=1)
    m_prev = m_ref[...]
    m_new = jnp.maximum(m_prev, jnp.max(s_all, axis=1, keepdims=True))
    alpha = jnp.exp(m_prev - m_new)
    p = jnp.exp(s_all - m_new)
    l_ref[...] = alpha * l_ref[...] + jnp.sum(p, axis=1, keepdims=True)
    v_all = jnp.concatenate([v_refs[i][...] for i in range(P)], axis=0).astype(BF16)
    acc_ref[...] = alpha * acc_ref[...] + jnp.dot(p.astype(BF16), v_all, preferred_element_type=F32)
    m_ref[...] = m_new

    @pl.when(c == nc - 1)
    def _():
        o_ref[0] = acc_ref[...] / l_ref[...]


def _fox_decode(q, kx, vx, zg, b_f, k_pages, v_pages, lf_pages, page_table, layer, *, KV, G, D):
    NB, n_pages = page_table.shape
    H = KV * G
    P = min(FOX_PAGES_PER_STEP, n_pages)
    assert n_pages % P == 0 and k_pages.shape[2] == LANES * KV
    nc = n_pages // P

    def page_map(i):
        return lambda b, c, pt: (layer, pt[b, n_pages - 1 - (c * P + i)], 0, 0)

    head = lambda b, c, pt: (b, 0, 0)
    k_specs = [pl.BlockSpec((None, None, LANES * KV, D), page_map(i)) for i in range(P)]
    lf_specs = [pl.BlockSpec((None, None, H, LANES), page_map(i)) for i in range(P)]
    kern = functools.partial(_fox_decode_kernel, P=P, KV=KV, G=G, nc=nc, scale=D ** -0.5)
    return pl.pallas_call(
        kern,
        grid_spec=pltpu.PrefetchScalarGridSpec(
            num_scalar_prefetch=1,
            grid=(NB, nc),
            in_specs=[pl.BlockSpec((1, H, D), head)] * 3
            + [pl.BlockSpec((1, H, 1), head), pl.BlockSpec((H, 1), lambda b, c, pt: (0, 0))]
            + k_specs + k_specs + lf_specs,
            out_specs=[pl.BlockSpec((1, H, D), head), pl.BlockSpec((1, H, 1), head)],
            scratch_shapes=[pltpu.VMEM((H, 1), F32)] * 3 + [pltpu.VMEM((H, D), F32)],
        ),
        out_shape=[jax.ShapeDtypeStruct((NB, H, D), F32), jax.ShapeDtypeStruct((NB, H, 1), F32)],
        compiler_params=_params("arbitrary", "arbitrary"),
        name="fox_decode",
    )(page_table, q, kx, vx, zg, b_f.reshape(H, 1), *([k_pages] * P), *([v_pages] * P), *([lf_pages] * P))


def _diff_decode_kernel(pt_ref, q_ref, kx_ref, vx_ref, sl_ref, lam_ref, g_ref, *rest,
                        P, KV, G, nc, scale, past, lam_init):
    k_refs, v_refs = rest[:P], rest[P:2 * P]
    o_ref, m_ref, l_ref, acc_lo_ref, acc_hi_ref = rest[2 * P:]
    c = pl.program_id(1)
    H = KV * G
    R = 2 * H
    q = q_ref[0] * scale

    @pl.when(c == 0)
    def _():
        m_ref[...] = jnp.sum(q * kx_ref[0], axis=1, keepdims=True)
        l_ref[...] = jnp.ones(l_ref.shape, F32)
        acc_lo_ref[...] = vx_ref[0, :, 0:LANES]
        acc_hi_ref[...] = vx_ref[0, :, LANES:2 * LANES]

    T = 2 * KV
    C = LANES * T
    q_b = q.astype(BF16)
    row = lax.broadcasted_iota(jnp.int32, (R, C), 0)
    col = lax.broadcasted_iota(jnp.int32, (R, C), 1)
    own = (col % T) == (row // H) * KV + (row % H) // G
    pos_in_page = lax.broadcasted_iota(jnp.int32, (1, C), 1) // T
    s_list = []
    for i in range(P):
        s = lax.dot_general(q_b, k_refs[i][...].astype(BF16), NT_DIMS, preferred_element_type=F32)
        rel = (past - ((c * P + i) * LANES + pos_in_page)).astype(F32)
        s_list.append(jnp.where(own, s - sl_ref[...] * rel, NEG))
    s_all = jnp.concatenate(s_list, axis=1)
    m_prev = m_ref[...]
    m_new = jnp.maximum(m_prev, jnp.max(s_all, axis=1, keepdims=True))
    alpha = jnp.exp(m_prev - m_new)
    lane_row = lax.broadcasted_iota(jnp.int32, (R, LANES), 0)
    lane_half = (lax.broadcasted_iota(jnp.int32, (R, LANES), 1) % T) // KV
    p_sum = jnp.zeros((R, LANES), F32)
    lo_tiles, hi_tiles = [], []
    for t in range(P * T):
        p = jnp.exp(s_all[:, t * LANES:(t + 1) * LANES] - m_new)
        p_sum = p_sum + p
        p = p + jnp.where(lane_row < H, pltpu.roll(p, KV, 1), pltpu.roll(p, LANES - KV, 1))
        lo_tiles.append(jnp.where(lane_half == 0, p, 0.0))
        hi_tiles.append(jnp.where(lane_half == 1, p, 0.0))
    l_ref[...] = alpha * l_ref[...] + jnp.sum(p_sum, axis=1, keepdims=True)
    p_both = jnp.concatenate([jnp.concatenate(lo_tiles, axis=1), jnp.concatenate(hi_tiles, axis=1)],
                             axis=0).astype(BF16)
    v_all = jnp.concatenate([v_refs[i][...] for i in range(P)], axis=0).astype(BF16)
    o = jnp.dot(p_both, v_all, preferred_element_type=F32)
    acc_lo_ref[...] = alpha * acc_lo_ref[...] + o[0:R]
    acc_hi_ref[...] = alpha * acc_hi_ref[...] + o[R:2 * R]
    m_ref[...] = m_new

    @pl.when(c == nc - 1)
    def _():
        lam = _diff_lambda(lam_ref, lam_init)
        inv = 1.0 / l_ref[...]
        lo = acc_lo_ref[...] * inv
        hi = acc_hi_ref[...] * inv
        o = jnp.concatenate([lo[0:H] - lam * lo[H:R], hi[0:H] - lam * hi[H:R]], axis=1)
        o_ref[0] = _rms(o, g_ref[...]) * (1.0 - lam_init)


def _diff_decode(q, kx, vx, slopes_rows, lam_rows, subln, k_pages, v_pages, page_table, layer, *, KV, G, D,
                 lam_init):
    NB, n_pages = page_table.shape
    H = KV * G
    R = 2 * H
    P = min(DIFF_PAGES_PER_STEP, n_pages)
    assert n_pages % P == 0 and k_pages.shape[2] == LANES * 2 * KV
    nc = n_pages // P

    def page_map(i):
        return lambda b, c, pt: (layer, pt[b, c * P + i], 0, 0)

    head = lambda b, c, pt: (b, 0, 0)
    const = lambda b, c, pt: (0, 0)
    k_specs = [pl.BlockSpec((None, None, LANES * 2 * KV, D), page_map(i)) for i in range(P)]
    kern = functools.partial(_diff_decode_kernel, P=P, KV=KV, G=G, nc=nc, scale=D ** -0.5,
                             past=n_pages * LANES, lam_init=lam_init)
    return pl.pallas_call(
        kern,
        grid_spec=pltpu.PrefetchScalarGridSpec(
            num_scalar_prefetch=1,
            grid=(NB, nc),
            in_specs=[pl.BlockSpec((1, R, D), head), pl.BlockSpec((1, R, D), head), pl.BlockSpec((1, R, 2 * D), head),
                      pl.BlockSpec((R, 1), const), pl.BlockSpec((4, D), const), pl.BlockSpec((1, 2 * D), const)]
            + k_specs + k_specs,
            out_specs=pl.BlockSpec((1, H, 2 * D), head),
            scratch_shapes=[pltpu.VMEM((R, 1), F32)] * 2 + [pltpu.VMEM((R, D), F32)] * 2,
        ),
        out_shape=jax.ShapeDtypeStruct((NB, H, 2 * D), F32),
        compiler_params=_params("arbitrary", "arbitrary"),
        name="diff_decode",
    )(page_table, q, kx, vx, slopes_rows, lam_rows, subln.reshape(1, 2 * D), *([k_pages] * P), *([v_pages] * P))


def _swa_decode_kernel(q_ref, kt_ref, vt_ref, kx_ref, vx_ref, kc_ref, vc_ref, sl_ref, sk_ref, o_ref, ko_ref, vo_ref,
                       *, H, KV, D, W, scale):
    G = H // KV
    newest = lax.broadcasted_iota(jnp.int32, (1, W), 1) == W - 1
    ko_ref[0] = jnp.where(newest, kc_ref[0], pltpu.roll(kt_ref[0], W - 1, 1))
    vo_ref[0] = jnp.where(newest, vc_ref[0], pltpu.roll(vt_ref[0], W - 1, 1))
    q = q_ref[0] * scale
    s = jnp.dot(q.astype(BF16), kt_ref[0].astype(BF16), preferred_element_type=F32)
    rel = (W - lax.broadcasted_iota(jnp.int32, (1, W), 1)).astype(F32)
    s = s - sl_ref[...] * rel
    s_new = jnp.sum(q * kx_ref[0], axis=1, keepdims=True)
    snk = sk_ref[...]
    m = jnp.maximum(jnp.maximum(jnp.max(s, axis=1, keepdims=True), s_new), snk)
    p = jnp.exp(s - m)
    p_new = jnp.exp(s_new - m)
    den = jnp.sum(p, axis=1, keepdims=True) + p_new + jnp.exp(snk - m)
    o = lax.dot_general((p / den).astype(BF16), vt_ref[0].astype(BF16), NT_DIMS, preferred_element_type=F32)
    o = o + (p_new / den) * vx_ref[0]
    col_kv = lax.broadcasted_iota(jnp.int32, (H, KV * D), 1) // D
    row_kv = lax.broadcasted_iota(jnp.int32, (H, KV * D), 0) // G
    o = jnp.where(col_kv == row_kv, o, 0.0)
    folded = o[:, 0:LANES]
    for t in range(1, KV * D // LANES):
        folded = folded + o[:, t * LANES:(t + 1) * LANES]
    odd = (lax.broadcasted_iota(jnp.int32, (H, LANES), 0) // G) % 2 == 1
    o_ref[0] = jnp.where(odd, pltpu.roll(folded, LANES // 2, 1), folded)


def _swa_decode(q_exp, k_t, v_t, kx, vx, k_col, v_col, slopes, sinks, *, H, KV, D, W):
    NB = q_exp.shape[0]
    kern = functools.partial(_swa_decode_kernel, H=H, KV=KV, D=D, W=W, scale=D ** -0.5)
    row = lambda b: (b, 0, 0)
    win = pl.BlockSpec((1, KV * D, W), row)
    return pl.pallas_call(
        kern,
        grid=(NB,),
        in_specs=[pl.BlockSpec((1, H, KV * D), row), win, win, pl.BlockSpec((1, H, KV * D), row),
                  pl.BlockSpec((1, H, KV * D), row),
                  pl.BlockSpec((1, KV * D, 1), row), pl.BlockSpec((1, KV * D, 1), row),
                  pl.BlockSpec((H, 1), lambda b: (0, 0)), pl.BlockSpec((H, 1), lambda b: (0, 0))],
        out_specs=[pl.BlockSpec((1, H, LANES), row), win, win],
        out_shape=[jax.ShapeDtypeStruct((NB, H, LANES), F32), jax.ShapeDtypeStruct((NB, KV * D, W), F32),
                   jax.ShapeDtypeStruct((NB, KV * D, W), F32)],
        compiler_params=_params("arbitrary"),
        name="swa_decode",
    )(q_exp, k_t, v_t, kx, vx, k_col, v_col, slopes.reshape(H, 1), sinks.reshape(H, 1))


def _cross_decode_kernel(q_ref, k_ref, v_ref, o_ref, *, CH, SB, scale):
    for b in range(SB):
        q_b = (q_ref[b] * scale).astype(BF16)
        s = lax.dot_general(q_b, k_ref[b].astype(BF16), NT_DIMS, preferred_element_type=F32)
        row = lax.broadcasted_iota(jnp.int32, s.shape, 0)
        col = lax.broadcasted_iota(jnp.int32, s.shape, 1)
        s = jnp.where(col % CH == row, s, NEG)
        m = jnp.max(s, axis=1, keepdims=True)
        p = jnp.where(col % CH == row, jnp.exp(s - m), 0.0)
        den = jnp.maximum(jnp.sum(p, axis=1, keepdims=True), 1e-30)
        o = jnp.dot((p / den).astype(BF16), v_ref[b].astype(BF16), preferred_element_type=F32)
        o_ref[b] = o[0:CH]


def _cross_decode(q_pad, mem_k, mem_v, layer, *, CH, D):
    NB, R, _ = q_pad.shape
    n = mem_k.shape[2]
    SB = 4 if NB % 4 == 0 else 1
    kern = functools.partial(_cross_decode_kernel, CH=CH, SB=SB, scale=D ** -0.5)
    return pl.pallas_call(
        kern,
        grid=(NB // SB,),
        in_specs=[pl.BlockSpec((SB, R, D), lambda b: (b, 0, 0)),
                  pl.BlockSpec((None, SB, n, D), lambda b: (layer, b, 0, 0)),
                  pl.BlockSpec((None, SB, n, D), lambda b: (layer, b, 0, 0))],
        out_specs=pl.BlockSpec((SB, CH, D), lambda b: (b, 0, 0)),
        out_shape=jax.ShapeDtypeStruct((NB, CH, D), F32),
        compiler_params=_params("arbitrary"),
        name="cross_decode",
    )(q_pad, mem_k, mem_v)


def _alibi_slopes(n):
    return jnp.exp2(-8.0 * jnp.arange(1, n + 1, dtype=F32) / n)


def kernel(x_prompt, x_sample, mem_prompt, cache_fox_k, cache_fox_v, cache_fox_logf, cache_diff_k, cache_diff_v, state_swa_k, state_swa_v, cache_mem_k, cache_mem_v, page_table, norm_mix, w_a_in, b_a_f, w_a_out, w_b_in, lam_q1, lam_k1, lam_q2, lam_k2, subln_b, w_b_out, w_c_in, b_c_in, sinks_c, w_c_out, norm_cross, norm_mem, w_x_q, w_x_kv, w_x_out, norm_ffn, w_ff_gu, w_ff_down, norm_final):
    B, S, dm = x_prompt.shape
    NB = x_sample.shape[0]
    depth = norm_mix.shape[0]
    n_mem = mem_prompt.shape[1]
    _, n_pool, page, KVA, HDA = cache_fox_k.shape
    HA = cache_fox_logf.shape[-1]
    KVB, HDB = cache_diff_k.shape[3], cache_diff_k.shape[4] // 2
    HB = dm // (2 * HDB)
    _, _, W, KVC, HDC = state_swa_k.shape
    HC = dm // HDC
    CH, CHD = cache_mem_k.shape[3], cache_mem_k.shape[4]
    assert page == LANES and x_sample.shape[1] == 1

    xp = x_prompt.reshape(B * S, dm)
    xs = x_sample.reshape(NB, dm)
    mem = mem_prompt.reshape(B * n_mem, dm)

    w_a_in, w_b_in, w_c_in = w_a_in.astype(BF16), w_b_in.astype(BF16), w_c_in.astype(BF16)
    w_a_out, w_b_out, w_c_out = w_a_out.astype(BF16), w_b_out.astype(BF16), w_c_out.astype(BF16)
    w_ff_down = w_ff_down.astype(BF16)

    fox_k_pages = cache_fox_k.reshape(-1, n_pool, page * KVA, HDA)
    fox_v_pages = cache_fox_v.reshape(-1, n_pool, page * KVA, HDA)
    fox_lf_pages = jnp.swapaxes(cache_fox_logf, 2, 3)

    def diff_pages(c):
        c = c.reshape(-1, n_pool, page, KVB, 2, HDB)
        return jnp.transpose(c, (0, 1, 2, 4, 3, 5)).reshape(-1, n_pool, page * 2 * KVB, HDB)

    diff_k_pages, diff_v_pages = diff_pages(cache_diff_k), diff_pages(cache_diff_v)
    mem_k_rows = cache_mem_k.reshape(depth, NB, n_mem * CH, CHD)
    mem_v_rows = cache_mem_v.reshape(depth, NB, n_mem * CH, CHD)

    slopes_b = _alibi_slopes(HB)
    slopes_c = _alibi_slopes(HC)

    n_a, n_b = w_a_in.shape[0], w_b_in.shape[0]
    pfk = pfv = pdk = pdv = None
    pfl, sfk, sfv, sfl = [], [], [], []
    sdk, sdv = [], []
    pck, pcv, sck, scv = [], [], [], []
    pmk = pmv = None
    ia = ib = ic = 0
    for i in range(depth):
        kind = i % 3
        if kind == 0:
            nq, nkv = HA * HDA, KVA * HDA
            G = HA // KVA
            z, pfk, pfv = _mm(xp, w_a_in, ia, gain=norm_mix[i],
                              lines=[(pfk, n_a, ia, nq, list(range(KVA)), KVA),
                                     (pfv, n_a, ia, nq + nkv, list(range(KVA)), KVA)])
            zs = _mm(xs, w_a_in, ia, gain=norm_mix[i])
            zg_t = jnp.swapaxes(z[:, nq + 2 * nkv:].reshape(B, S, HA), 1, 2)
            lf_t, c_t = _fox_gate(zg_t, b_a_f[ia])
            o = _fox_flash(z, c_t.reshape(B, KVA, G, S), B, S, H=HA, KV=KVA, D=HDA)
            xp = _mm(o, w_a_out, ia, residual=xp, tm=2048)
            pfl.append(jnp.swapaxes(lf_t, 1, 2))

            ks = zs[:, nq:nq + nkv].reshape(NB, KVA, HDA)
            vs = zs[:, nq + nkv:nq + 2 * nkv].reshape(NB, KVA, HDA)
            os_, lfs = _fox_decode(
                zs[:, :nq].reshape(NB, HA, HDA), jnp.repeat(ks, G, axis=1), jnp.repeat(vs, G, axis=1),
                zs[:, nq + 2 * nkv:].reshape(NB, HA, 1), b_a_f[ia],
                fox_k_pages, fox_v_pages, fox_lf_pages, page_table, ia, KV=KVA, G=G, D=HDA)
            xs = _mm(os_.reshape(NB, nq), w_a_out, ia, residual=xs)
            sfk.append(ks.reshape(NB, 1, KVA, HDA))
            sfv.append(vs.reshape(NB, 1, KVA, HDA))
            sfl.append(lfs.reshape(NB, 1, HA))
            ia += 1
        elif kind == 1:
            lam_init = 0.8 - 0.6 * math.exp(-0.3 * i)
            nq, nkv = HB * 2 * HDB, KVB * 2 * HDB
            G = HB // KVB
            lam_rows = jnp.stack([lam_q1[ib], lam_k1[ib], lam_q2[ib], lam_k2[ib]])
            half_kv = [(c % 2) * KVB + c // 2 for c in range(2 * KVB)]
            z, pdk, pdv = _mm(xp, w_b_in, ib, gain=norm_mix[i],
                              lines=[(pdk, n_b, ib, nq, half_kv, 2 * KVB),
                                     (pdv, n_b, ib, nq + nkv, half_kv, 2 * KVB)])
            zs = _mm(xs, w_b_in, ib, gain=norm_mix[i])
            o = _diff_flash(z, slopes_b, lam_rows, subln_b[ib], B, S, H=HB, KV=KVB, D=HDB, lam_init=lam_init)
            xp = _mm(o, w_b_out, ib, residual=xp, tm=2048)

            ks = zs[:, nq:nq + nkv].reshape(NB, KVB, 2, HDB)
            vs = zs[:, nq + nkv:].reshape(NB, KVB, 2 * HDB)
            q_rows = jnp.transpose(zs[:, :nq].reshape(NB, HB, 2, HDB), (0, 2, 1, 3)).reshape(NB, 2 * HB, HDB)
            kx = jnp.transpose(jnp.repeat(ks, G, axis=1), (0, 2, 1, 3)).reshape(NB, 2 * HB, HDB)
            vx = jnp.tile(jnp.repeat(vs, G, axis=1), (1, 2, 1))
            os_ = _diff_decode(q_rows, kx, vx, jnp.tile(slopes_b, 2).reshape(2 * HB, 1), lam_rows, subln_b[ib],
                               diff_k_pages, diff_v_pages, page_table, ib, KV=KVB, G=G, D=HDB, lam_init=lam_init)
            xs = _mm(os_.reshape(NB, nq), w_b_out, ib, residual=xs)
            sdk.append(ks.reshape(NB, 1, KVB, 2 * HDB))
            sdv.append(vs.reshape(NB, 1, KVB, 2 * HDB))
            ib += 1
        else:
            nq, nkv = HC * HDC, KVC * HDC
            G = HC // KVC
            z = _mm(xp, w_c_in, ic, gain=norm_mix[i], bias=b_c_in[ic])
            zs = _mm(xs, w_c_in, ic, gain=norm_mix[i], bias=b_c_in[ic])
            o = _swa_flash(z, slopes_c, sinks_c[ic], B, S, H=HC, KV=KVC, D=HDC, W=W)
            xp = _mm(o, w_c_out, ic, residual=xp, tm=2048)
            z3 = z.reshape(B, S, nq + 2 * nkv)
            pck.append(z3[:, S - W:, nq:nq + nkv].reshape(B, W, KVC, HDC))
            pcv.append(z3[:, S - W:, nq + nkv:].reshape(B, W, KVC, HDC))

            blk = (jnp.arange(HC)[:, None] // G == jnp.arange(KVC)[None, :]).astype(F32)
            q_exp = (zs[:, :nq].reshape(NB, HC, 1, HDC) * blk[None, :, :, None]).reshape(NB, HC, nkv)
            kx = jnp.broadcast_to(zs[:, None, nq:nq + nkv], (NB, HC, nkv))
            vx = jnp.broadcast_to(zs[:, None, nq + nkv:], (NB, HC, nkv))
            k_t = jnp.transpose(state_swa_k[ic], (0, 2, 3, 1)).reshape(NB, nkv, W)
            v_t = jnp.transpose(state_swa_v[ic], (0, 2, 3, 1)).reshape(NB, nkv, W)
            o2, k_win, v_win = _swa_decode(
                q_exp, k_t, v_t, kx, vx, zs[:, nq:nq + nkv].reshape(NB, nkv, 1), zs[:, nq + nkv:].reshape(NB, nkv, 1),
                slopes_c, sinks_c[ic], H=HC, KV=KVC, D=HDC, W=W)
            xs = _mm(o2[:, :, :HDC].reshape(NB, nq), w_c_out, ic, residual=xs)
            sck.append(jnp.transpose(k_win.reshape(NB, KVC, HDC, W), (0, 3, 1, 2)))
            scv.append(jnp.transpose(v_win.reshape(NB, KVC, HDC, W), (0, 3, 1, 2)))
            ic += 1

        mkv, pmk, pmv = _mm(mem, w_x_kv, i, gain=norm_mem[i],
                            lines=[(pmk, depth, i, 0, list(range(CH)), CH),
                                   (pmv, depth, i, CH * CHD, list(range(CH)), CH)])
        xp = _cross_prompt(xp, norm_cross[i], w_x_q, w_x_out, i, mkv, B, S, CH=CH, D=CHD)
        qs = _mm(xs, w_x_q, i, gain=norm_cross[i]).reshape(NB, CH, CHD)
        q_pad = jnp.pad(qs, ((0, 0), (0, 16 - CH), (0, 0)))
        oc = _cross_decode(q_pad, mem_k_rows, mem_v_rows, i, CH=CH, D=CHD)
        xs = _mm(oc.reshape(NB, CH * CHD), w_x_out, i, residual=xs)

        h = _mm(xp, w_ff_gu, i, gain=norm_ffn[i], glu=True, out_dtype=BF16)
        xp = _mm(h, w_ff_down, i, residual=xp)
        hs = _mm(xs, w_ff_gu, i, gain=norm_ffn[i], glu=True, out_dtype=BF16)
        xs = _mm(hs, w_ff_down, i, residual=xs)

    y_prompt = _rmsnorm(xp, norm_final).reshape(B, S, dm)
    y_sample = _rmsnorm(xs, norm_final).reshape(NB, 1, dm)
    def diff_rows(lines):
        x = lines.reshape(n_b, B, S, 2, KVB, HDB)
        return jnp.transpose(x, (0, 1, 2, 4, 3, 5)).reshape(n_b, B, S, KVB, 2 * HDB)

    return (y_prompt, y_sample,
            pfk.reshape(n_a, B, S, KVA, HDA), pfv.reshape(n_a, B, S, KVA, HDA), jnp.stack(pfl),
            diff_rows(pdk), diff_rows(pdv),
            jnp.stack(pck), jnp.stack(pcv),
            pmk.reshape(depth, B, n_mem, CH, CHD), pmv.reshape(depth, B, n_mem, CH, CHD),
            jnp.stack(sfk), jnp.stack(sfv), jnp.stack(sfl),
            jnp.stack(sdk), jnp.stack(sdv),
            jnp.stack(sck), jnp.stack(scv))
```
